```python
import jax, jax.numpy as jnp
from jax import lax
import numpy as np

D_MODEL = 1024
BATCH = 8
SEQ = 4096
DEPTH = 2
DEC_BATCH = 32
DEC_SEQ = 4
PAST_LEN = 16384
PAGE_SIZE = 128

N_A_LAYERS = DEPTH // 2
N_B_LAYERS = DEPTH - N_A_LAYERS
DN_ALPHA = (2.0 * DEPTH) ** 0.25
DN_BETA = (8.0 * DEPTH) ** -0.25
LN_EPS = 1e-5

M_HEADS = 8
M_DK = D_MODEL // M_HEADS // 2
M_DV = D_MODEL // M_HEADS
M_CHUNK = 64
M_IN_COLS = 2 * M_HEADS * M_DK + 2 * M_HEADS * M_DV + 2 * M_HEADS

A_HEADS = 16
A_HD = D_MODEL // A_HEADS
ROT_DIM = A_HD // 4
ROPE_THETA = 500000.0
MOBA_BLOCK = 256
MOBA_TOPK = 3
MOBA_Q_CHUNK = 16

P_HEADS = 8
P_NKEYS = 128
P_EXPERTS = P_NKEYS * P_NKEYS
P_DQ = 256
P_TOPK = 16
P_TOK_CHUNK = 256

kernel_name = "yoco_mlstm_moba_peer_step"

F32 = jnp.float32


def layer_norm(x, g, b):
    xf = x.astype(F32)
    mu = xf.mean(-1, keepdims=True)
    var = jnp.square(xf - mu).mean(-1, keepdims=True)
    return ((xf - mu) * lax.rsqrt(var + LN_EPS) * g.astype(F32) + b.astype(F32)).astype(x.dtype)


def rope_partial(x, pos):
    inv = ROPE_THETA ** (-jnp.arange(0, ROT_DIM, 2, dtype=F32) / ROT_DIM)
    ang = pos.astype(F32)[:, None] * inv[None, :]
    cos = jnp.cos(ang)[None, :, None, :]
    sin = jnp.sin(ang)[None, :, None, :]
    xr = x[..., :ROT_DIM].astype(F32)
    x1, x2 = xr[..., :ROT_DIM // 2], xr[..., ROT_DIM // 2:]
    rot = jnp.concatenate([x1 * cos - x2 * sin, x2 * cos + x1 * sin], axis=-1).astype(x.dtype)
    return jnp.concatenate([rot, x[..., ROT_DIM:]], axis=-1)


def mlstm_chunk(carry, blk):
    C, n, m = carry
    q, k, v, ig, lf = blk
    L = q.shape[2]
    b = jnp.cumsum(lf, axis=-1)
    causal = jnp.tril(jnp.ones((L, L), dtype=bool))
    dlog = jnp.where(causal, b[..., :, None] - b[..., None, :] + ig[..., None, :], -jnp.inf)
    m_inter = b + m[..., None]
    m_t = jnp.maximum(m_inter, dlog.max(-1))
    w_intra = jnp.exp(dlog - m_t[..., None])
    w_inter = jnp.exp(m_inter - m_t)
    s = jnp.einsum('bhtd,bhsd->bhts', q, k) * w_intra
    num = jnp.einsum('bhts,bhsv->bhtv', s, v) + w_inter[..., None] * jnp.einsum('bhtd,bhdv->bhtv', q, C)
    den = s.sum(-1) + w_inter * jnp.einsum('bhtd,bhd->bht', q, n)
    h = num / jnp.maximum(jnp.abs(den), jnp.exp(-m_t))[..., None]
    m_new = m_t[..., -1]
    decay = jnp.exp(b[..., -1] + m - m_new)
    w_s = jnp.exp(b[..., -1:] - b + ig - m_new[..., None])
    C_new = decay[..., None, None] * C + jnp.einsum('bhs,bhsd,bhsv->bhdv', w_s, k, v)
    n_new = decay[..., None] * n + jnp.einsum('bhs,bhsd->bhd', w_s, k)
    return (C_new, n_new, m_new), h


def mlstm_mixer(x, C0, n0, m0, w_in, b_gate, mh_gain, w_out):
    B, S, _ = x.shape
    qk, vd = M_HEADS * M_DK, M_HEADS * M_DV
    proj = x @ w_in
    q = proj[..., :qk]
    k = proj[..., qk:2 * qk]
    v = proj[..., 2 * qk:2 * qk + vd]
    o = proj[..., 2 * qk + vd:2 * qk + 2 * vd]
    gates = proj[..., 2 * qk + 2 * vd:].astype(F32) + b_gate.astype(F32)
    ig = gates[..., :M_HEADS].transpose(0, 2, 1)
    lf = jax.nn.log_sigmoid(gates[..., M_HEADS:]).transpose(0, 2, 1)

    def heads(t, d):
        return t.reshape(B, S, M_HEADS, d).transpose(0, 2, 1, 3).astype(F32)

    q = heads(q, M_DK) * (M_DK ** -0.5)
    k = heads(k, M_DK)
    v = heads(v, M_DV)
    L = min(M_CHUNK, S)
    nc = S // L

    def to_chunks(t):
        return jnp.moveaxis(t.reshape(t.shape[:2] + (nc, L) + t.shape[3:]), 2, 0)

    carry0 = (C0.astype(F32), n0.astype(F32), m0.astype(F32))
    (C, n, m), h = lax.scan(mlstm_chunk, carry0,
                            (to_chunks(q), to_chunks(k), to_chunks(v), to_chunks(ig), to_chunks(lf)))
    h = jnp.moveaxis(h, 0, 2).reshape(B, M_HEADS, S, M_DV).transpose(0, 2, 1, 3)
    h = h * lax.rsqrt(jnp.mean(h * h, axis=-1, keepdims=True) + 1e-6)
    h = h.reshape(B, S, vd) * mh_gain.astype(F32) * jax.nn.sigmoid(o.astype(F32))
    return h.astype(x.dtype) @ w_out, C, n, m


def moba_prompt(q, k, v):
    B, S, H, hd = q.shape
    nblk = -(-S // MOBA_BLOCK)
    pad = nblk * MOBA_BLOCK - S
    kp = jnp.pad(k, ((0, 0), (0, pad), (0, 0), (0, 0)))
    vp = jnp.pad(v, ((0, 0), (0, pad), (0, 0), (0, 0)))
    k_blk = kp.reshape(B, nblk, MOBA_BLOCK, H, hd)
    v_blk = vp.reshape(B, nblk, MOBA_BLOCK, H, hd)
    k_mean = k_blk.astype(F32).mean(axis=2)
    topk = min(MOBA_TOPK, nblk)
    scale = hd ** -0.5
    b_idx = jnp.arange(B)[:, None, None, None]
    h_idx = jnp.arange(H)[None, None, :, None]

    def one_chunk(c):
        start = c * MOBA_Q_CHUNK
        qc = lax.dynamic_slice_in_dim(q, start, MOBA_Q_CHUNK, axis=1).astype(F32)
        cur = start // MOBA_BLOCK
        gate = jnp.einsum('bqhd,bnhd->bqhn', qc, k_mean)
        gate = jnp.where(jnp.arange(nblk) < cur, gate, -jnp.inf)
        _, sel = lax.top_k(gate, topk)
        valid = sel < cur
        kg = k_blk[b_idx, sel, :, h_idx].astype(F32)
        vg = v_blk[b_idx, sel, :, h_idx].astype(F32)
        ls = jnp.einsum('bqhd,bqhkjd->bqhkj', qc, kg) * scale
        ls = jnp.where(valid[..., None], ls, -jnp.inf).reshape(B, MOBA_Q_CHUNK, H, topk * MOBA_BLOCK)
        k_own = lax.dynamic_slice_in_dim(kp, cur * MOBA_BLOCK, MOBA_BLOCK, axis=1).astype(F32)
        v_own = lax.dynamic_slice_in_dim(vp, cur * MOBA_BLOCK, MOBA_BLOCK, axis=1).astype(F32)
        lo = jnp.einsum('bqhd,bjhd->bqhj', qc, k_own) * scale
        qpos = start + jnp.arange(MOBA_Q_CHUNK)
        kpos = cur * MOBA_BLOCK + jnp.arange(MOBA_BLOCK)
        lo = jnp.where((kpos[None, :] <= qpos[:, None])[None, :, None, :], lo, -jnp.inf)
        p = jax.nn.softmax(jnp.concatenate([ls, lo], axis=-1), axis=-1)
        ps = p[..., :topk * MOBA_BLOCK].reshape(B, MOBA_Q_CHUNK, H, topk, MOBA_BLOCK)
        po = p[..., topk * MOBA_BLOCK:]
        out = jnp.einsum('bqhkj,bqhkjd->bqhd', ps, vg) + jnp.einsum('bqhj,bjhd->bqhd', po, v_own)
        return out.astype(q.dtype)

    outs = lax.map(one_chunk, jnp.arange(S // MOBA_Q_CHUNK))
    return jnp.moveaxis(outs, 0, 1).reshape(B, S, H, hd)


def moba_sample(q, k_new, v_new, cache_k, cache_v, page_table):
    Bd, T, H, hd = q.shape
    ppb = MOBA_BLOCK // PAGE_SIZE
    n_past_blk = PAST_LEN // MOBA_BLOCK
    own_pages = (PAST_LEN % MOBA_BLOCK) // PAGE_SIZE
    scale = hd ** -0.5
    qf = q.astype(F32)
    own_pt = page_table[:, n_past_blk * ppb:n_past_blk * ppb + own_pages]
    R = own_pages * PAGE_SIZE
    k_own = jnp.concatenate([cache_k[own_pt].reshape(Bd, R, H, hd), k_new], axis=1).astype(F32)
    v_own = jnp.concatenate([cache_v[own_pt].reshape(Bd, R, H, hd), v_new], axis=1).astype(F32)
    own_mask = jnp.concatenate([jnp.ones((T, R), dtype=bool), jnp.tril(jnp.ones((T, T), dtype=bool))], axis=1)
    lo = jnp.einsum('bthd,bjhd->bthj', qf, k_own) * scale
    lo = jnp.where(own_mask[None, :, None, :], lo, -jnp.inf)
    if n_past_blk > 0:
        topk = min(MOBA_TOPK, n_past_blk)
        blk_pt = page_table[:, :n_past_blk * ppb]
        k_mean = cache_k[blk_pt].astype(F32).reshape(Bd, n_past_blk, ppb * PAGE_SIZE, H, hd).mean(axis=2)
        gate = jnp.einsum('bthd,bnhd->bthn', qf, k_mean)
        _, sel = lax.top_k(gate, topk)
        b_idx = jnp.arange(Bd)[:, None, None, None, None]
        h_idx = jnp.arange(H)[None, None, :, None, None]
        phys = page_table[b_idx, sel[..., None] * ppb + jnp.arange(ppb)]
        kg = cache_k[phys, :, h_idx].astype(F32).reshape(Bd, T, H, topk * MOBA_BLOCK, hd)
        vg = cache_v[phys, :, h_idx].astype(F32).reshape(Bd, T, H, topk * MOBA_BLOCK, hd)
        ls = jnp.einsum('bthd,bthjd->bthj', qf, kg) * scale
        p = jax.nn.softmax(jnp.concatenate([ls, lo], axis=-1), axis=-1)
        nsel = topk * MOBA_BLOCK
        out = jnp.einsum('bthj,bthjd->bthd', p[..., :nsel], vg) + jnp.einsum('bthj,bjhd->bthd', p[..., nsel:], v_own)
    else:
        p = jax.nn.softmax(lo, axis=-1)
        out = jnp.einsum('bthj,bjhd->bthd', p, v_own)
    return out.astype(q.dtype)


def peer_ffn(x, w_q, sub_keys, u_tab, v_tab):
    shp = x.shape
    xt = x.reshape(-1, D_MODEL)
    n = xt.shape[0]
    n_pad = -(-n // P_TOK_CHUNK) * P_TOK_CHUNK
    xt = jnp.pad(xt, ((0, n_pad - n), (0, 0)))

    def one(xc):
        q = (xc @ w_q).reshape(-1, P_HEADS, 2, P_DQ // 2).astype(F32)
        s = jnp.einsum('nhpd,hpkd->nhpk', q, sub_keys.astype(F32))
        sv, si = lax.top_k(s, P_TOPK)
        cand = (sv[..., 0, :, None] + sv[..., 1, None, :]).reshape(-1, P_HEADS, P_TOPK * P_TOPK)
        cv, ci = lax.top_k(cand, P_TOPK)
        e = (jnp.take_along_axis(si[..., 0, :], ci // P_TOPK, axis=-1) * P_NKEYS
             + jnp.take_along_axis(si[..., 1, :], ci % P_TOPK, axis=-1))
        g = jax.nn.softmax(cv, axis=-1)
        u = u_tab[e].astype(F32)
        hid = jax.nn.gelu(jnp.einsum('nd,nhkd->nhk', xc.astype(F32), u), approximate=False)
        return jnp.einsum('nhk,nhkd->nd', g * hid, v_tab[e].astype(F32)).astype(x.dtype)

    out = lax.map(one, xt.reshape(-1, P_TOK_CHUNK, D_MODEL))
    return out.reshape(n_pad, D_MODEL)[:n].reshape(shp)


def setup_inputs(seed: int = 0) -> dict:
    key = jax.random.key(seed)
    ks = jax.random.split(key, 32)
    D = D_MODEL
    n_pages = PAST_LEN // PAGE_SIZE
    n_used = DEC_BATCH * n_pages
    n_phys = n_used + n_used // 4

    def nrm(k, shape, s):
        return jax.random.normal(k, shape, F32) * s

    sd = D ** -0.5
    qk, vd = M_HEADS * M_DK, M_HEADS * M_DV
    w_in_a = jnp.concatenate([
        nrm(ks[10], (N_A_LAYERS, D, 2 * qk), sd),
        nrm(ks[11], (N_A_LAYERS, D, vd), sd * DN_BETA),
        nrm(ks[12], (N_A_LAYERS, D, vd), sd),
        nrm(ks[13], (N_A_LAYERS, D, 2 * M_HEADS), sd)], axis=-1)
    b_gate_a = jnp.concatenate([nrm(ks[14], (N_A_LAYERS, M_HEADS), 0.1),
                                3.0 + nrm(ks[15], (N_A_LAYERS, M_HEADS), 0.1)], axis=-1)
    w_kv = jnp.concatenate([nrm(ks[16], (D, A_HEADS * A_HD), sd),
                            nrm(ks[17], (D, A_HEADS * A_HD), sd * DN_BETA)], axis=-1)
    return {
        "x_prompt": nrm(ks[0], (BATCH, SEQ, D), 1.0),
        "x_sample": nrm(ks[1], (DEC_BATCH, DEC_SEQ, D), 1.0),
        "state_mlstm_C": nrm(ks[2], (N_A_LAYERS, DEC_BATCH, M_HEADS, M_DK, M_DV), 0.1),
        "state_mlstm_n": nrm(ks[3], (N_A_LAYERS, DEC_BATCH, M_HEADS, M_DK), 0.1),
        "state_mlstm_m": nrm(ks[4], (N_A_LAYERS, DEC_BATCH, M_HEADS), 1.0),
        "cache_k": nrm(ks[5], (n_phys, PAGE_SIZE, A_HEADS, A_HD), 1.0),
        "cache_v": nrm(ks[6], (n_phys, PAGE_SIZE, A_HEADS, A_HD), DN_BETA),
        "page_table": jax.random.permutation(ks[7], n_phys)[:n_used].reshape(DEC_BATCH, n_pages).astype(jnp.int32),
        "w_in_a": w_in_a,
        "b_gate_a": b_gate_a,
        "mh_norm_a": 1.0 + nrm(ks[18], (N_A_LAYERS, vd), 0.02),
        "w_out_a": nrm(ks[19], (N_A_LAYERS, vd, D), vd ** -0.5 * DN_BETA),
        "w_kv": w_kv,
        "w_q_b": nrm(ks[20], (N_B_LAYERS, D, A_HEADS * A_HD), sd),
        "w_out_b": nrm(ks[21], (N_B_LAYERS, A_HEADS * A_HD, D), (A_HEADS * A_HD) ** -0.5 * DN_BETA),
        "ln_mix_g": 1.0 + nrm(ks[22], (DEPTH, D), 0.02),
        "ln_mix_b": nrm(ks[23], (DEPTH, D), 0.02),
        "ln_ffn_g": 1.0 + nrm(ks[24], (DEPTH, D), 0.02),
        "ln_ffn_b": nrm(ks[25], (DEPTH, D), 0.02),
        "peer_wq": nrm(ks[26], (DEPTH, D, P_HEADS * P_DQ), sd),
        "peer_keys": nrm(ks[27], (DEPTH, P_HEADS, 2, P_NKEYS, P_DQ // 2), (P_DQ // 2) ** -0.5),
        "peer_u": nrm(ks[28], (DEPTH, P_EXPERTS, D), sd),
        "peer_v": nrm(ks[29], (DEPTH, P_EXPERTS, D), DN_BETA * P_HEADS ** -0.5),
    }


def reference(x_prompt, x_sample, state_mlstm_C, state_mlstm_n, state_mlstm_m, cache_k, cache_v, page_table,
              w_in_a, b_gate_a, mh_norm_a, w_out_a, w_kv, w_q_b, w_out_b,
              ln_mix_g, ln_mix_b, ln_ffn_g, ln_ffn_b, peer_wq, peer_keys, peer_u, peer_v):
    pos_p = jnp.arange(SEQ, dtype=jnp.int32)
    pos_s = PAST_LEN + jnp.arange(DEC_SEQ, dtype=jnp.int32)
    xp, xs = x_prompt, x_sample
    Cp, Np, Mp, Cs, Ns, Ms = [], [], [], [], [], []
    kp = vp = ks_ = vs_ = None
    for l in range(DEPTH):
        if l < N_A_LAYERS:
            a = l
            zC = jnp.zeros((BATCH, M_HEADS, M_DK, M_DV), F32)
            zN = jnp.zeros((BATCH, M_HEADS, M_DK), F32)
            zM = jnp.zeros((BATCH, M_HEADS), F32)
            yp, c1, n1, m1 = mlstm_mixer(xp, zC, zN, zM, w_in_a[a], b_gate_a[a], mh_norm_a[a], w_out_a[a])
            ys, c2, n2, m2 = mlstm_mixer(xs, state_mlstm_C[a], state_mlstm_n[a], state_mlstm_m[a],
                                         w_in_a[a], b_gate_a[a], mh_norm_a[a], w_out_a[a])
            Cp.append(c1); Np.append(n1); Mp.append(m1)
            Cs.append(c2); Ns.append(n2); Ms.append(m2)
        else:
            bl = l - N_A_LAYERS
            if bl == 0:
                kvp = xp @ w_kv
                kvs = xs @ w_kv
                hd_all = A_HEADS * A_HD
                kp = rope_partial(kvp[..., :hd_all].reshape(BATCH, SEQ, A_HEADS, A_HD), pos_p)
                vp = kvp[..., hd_all:].reshape(BATCH, SEQ, A_HEADS, A_HD)
                ks_ = rope_partial(kvs[..., :hd_all].reshape(DEC_BATCH, DEC_SEQ, A_HEADS, A_HD), pos_s)
                vs_ = kvs[..., hd_all:].reshape(DEC_BATCH, DEC_SEQ, A_HEADS, A_HD)
            qp = rope_partial((xp @ w_q_b[bl]).reshape(BATCH, SEQ, A_HEADS, A_HD), pos_p)
            qs = rope_partial((xs @ w_q_b[bl]).reshape(DEC_BATCH, DEC_SEQ, A_HEADS, A_HD), pos_s)
            yp = moba_prompt(qp, kp, vp).reshape(BATCH, SEQ, A_HEADS * A_HD) @ w_out_b[bl]
            ys = moba_sample(qs, ks_, vs_, cache_k, cache_v, page_table).reshape(
                DEC_BATCH, DEC_SEQ, A_HEADS * A_HD) @ w_out_b[bl]
        xp = layer_norm(DN_ALPHA * xp + yp, ln_mix_g[l], ln_mix_b[l])
        xs = layer_norm(DN_ALPHA * xs + ys, ln_mix_g[l], ln_mix_b[l])
        xp = layer_norm(DN_ALPHA * xp + peer_ffn(xp, peer_wq[l], peer_keys[l], peer_u[l], peer_v[l]),
                        ln_ffn_g[l], ln_ffn_b[l])
        xs = layer_norm(DN_ALPHA * xs + peer_ffn(xs, peer_wq[l], peer_keys[l], peer_u[l], peer_v[l]),
                        ln_ffn_g[l], ln_ffn_b[l])
    return (xp, xs, jnp.stack(Cp), jnp.stack(Np), jnp.stack(Mp), jnp.stack(Cs), jnp.stack(Ns), jnp.stack(Ms),
            kp, vp, ks_, vs_)
```

```python
import functools

import jax
import jax.numpy as jnp
from jax import lax
from jax.experimental import pallas as pl
from jax.experimental.pallas import tpu as pltpu

F32 = jnp.float32
BF16 = jnp.bfloat16
I32 = jnp.int32
HIGHEST = lax.Precision.HIGHEST

DEPTH = 2
DN_ALPHA = (2.0 * DEPTH) ** 0.25
LN_EPS = 1e-5

M_HEADS = 8
M_DK = 64
M_DV = 128
MLSTM_CHUNK = 256
MLSTM_SAMPLE_CHUNK = 128

A_HEADS = 16
A_HD = 64
ROT_DIM = 16
ROPE_THETA = 500000.0
MOBA_BLOCK = 256
MOBA_TOPK = 3
PAGE_SIZE = 128

P_HEADS = 8
P_NKEYS = 128
P_TOPK = 16
P_SLOTS = P_HEADS * P_TOPK

ROW_TILE = 512
TOK_BLOCK = 128
NEG = -1e30
VMEM_LIMIT = 56 * 1024 * 1024

NT_DIMS = (((1,), (1,)), ((), ()))


def _cparams(semantics, vmem=VMEM_LIMIT):
    return pltpu.CompilerParams(dimension_semantics=semantics, vmem_limit_bytes=vmem)


def _mm_kernel(x_ref, w_ref, o_ref):
    o_ref[...] = jnp.dot(x_ref[...].astype(BF16), w_ref[...], preferred_element_type=F32)


def _matmul(x, w_bf16, tm=ROW_TILE):
    m, k = x.shape
    n = w_bf16.shape[1]
    return pl.pallas_call(
        _mm_kernel,
        grid=(m // tm,),
        in_specs=[pl.BlockSpec((tm, k), lambda i: (i, 0)),
                  pl.BlockSpec((k, n), lambda i: (0, 0))],
        out_specs=pl.BlockSpec((tm, n), lambda i: (i, 0)),
        out_shape=jax.ShapeDtypeStruct((m, n), F32),
        compiler_params=_cparams(("parallel",)),
        name="matmul",
    )(x, w_bf16)


def _layer_norm_rows(z, g, b):
    mu = jnp.mean(z, axis=-1, keepdims=True)
    zc = z - mu
    var = jnp.mean(zc * zc, axis=-1, keepdims=True)
    return zc * lax.rsqrt(var + LN_EPS) * g + b


def _mm_ln_kernel(a_ref, w_ref, x_ref, g_ref, b_ref, o_ref):
    y = jnp.dot(a_ref[...].astype(BF16), w_ref[...], preferred_element_type=F32)
    o_ref[...] = _layer_norm_rows(DN_ALPHA * x_ref[...] + y, g_ref[...], b_ref[...])


def _matmul_res_ln(a, w_bf16, x_res, g, b, tm=ROW_TILE):
    m, k = a.shape
    n = w_bf16.shape[1]
    return pl.pallas_call(
        _mm_ln_kernel,
        grid=(m // tm,),
        in_specs=[pl.BlockSpec((tm, k), lambda i: (i, 0)),
                  pl.BlockSpec((k, n), lambda i: (0, 0)),
                  pl.BlockSpec((tm, n), lambda i: (i, 0)),
                  pl.BlockSpec((1, n), lambda i: (0, 0)),
                  pl.BlockSpec((1, n), lambda i: (0, 0))],
        out_specs=pl.BlockSpec((tm, n), lambda i: (i, 0)),
        out_shape=jax.ShapeDtypeStruct((m, n), F32),
        compiler_params=_cparams(("parallel",)),
        name="matmul_res_ln",
    )(a, w_bf16, x_res, g.reshape(1, n), b.reshape(1, n))


def _res_ln_kernel(x_ref, y_ref, g_ref, b_ref, o_ref):
    o_ref[...] = _layer_norm_rows(DN_ALPHA * x_ref[...] + y_ref[...], g_ref[...], b_ref[...])


def _res_ln(x, y, g, b, tm=ROW_TILE):
    m, n = x.shape
    return pl.pallas_call(
        _res_ln_kernel,
        grid=(m // tm,),
        in_specs=[pl.BlockSpec((tm, n), lambda i: (i, 0)),
                  pl.BlockSpec((tm, n), lambda i: (i, 0)),
                  pl.BlockSpec((1, n), lambda i: (0, 0)),
                  pl.BlockSpec((1, n), lambda i: (0, 0))],
        out_specs=pl.BlockSpec((tm, n), lambda i: (i, 0)),
        out_shape=jax.ShapeDtypeStruct((m, n), F32),
        compiler_params=_cparams(("parallel",)),
        name="res_ln",
    )(x, y, g.reshape(1, n), b.reshape(1, n))


def _gates_kernel(x_ref, w_ref, b_ref, o_ref):
    g = jnp.dot(x_ref[...], w_ref[...], precision=HIGHEST, preferred_element_type=F32) + b_ref[...]
    lane = lax.broadcasted_iota(I32, g.shape, 1)
    log_sig = jnp.minimum(g, 0.0) - jnp.log1p(jnp.exp(-jnp.abs(g)))
    o_ref[...] = jnp.where(lane < M_HEADS, g, log_sig)


def _mlstm_gates(x, w_gate, b_gate, tm=ROW_TILE):
    m, k = x.shape
    w = jnp.zeros((k, 128), F32).at[:, :2 * M_HEADS].set(w_gate)
    b = jnp.zeros((1, 128), F32).at[0, :2 * M_HEADS].set(b_gate)
    return pl.pallas_call(
        _gates_kernel,
        grid=(m // tm,),
        in_specs=[pl.BlockSpec((tm, k), lambda i: (i, 0)),
                  pl.BlockSpec((k, 128), lambda i: (0, 0)),
                  pl.BlockSpec((1, 128), lambda i: (0, 0))],
        out_specs=pl.BlockSpec((tm, 128), lambda i: (i, 0)),
        out_shape=jax.ShapeDtypeStruct((m, 128), F32),
        compiler_params=_cparams(("parallel",)),
        name="mlstm_gates",
    )(x, w, b)


def _mlstm_kernel(q_ref, kt_ref, v_ref, o_ref, gc_ref, gr_ref, c0_ref, m0_ref, gain_ref,
                  hn_ref, cout_ref, mout_ref, c_sc, m_sc):
    c = pl.program_id(1)
    L = q_ref.shape[0]

    @pl.when(c == 0)
    def _():
        c_sc[...] = c0_ref[0]
        m_sc[...] = m0_ref[0]

    gc = gc_ref[...]
    gr = gr_ref[0]
    row = lax.broadcasted_iota(I32, (L, L), 0)
    col = lax.broadcasted_iota(I32, (L, L), 1)
    causal = col <= row
    b_col = jnp.dot(causal.astype(F32), gc, precision=HIGHEST, preferred_element_type=F32)
    b_row = jnp.dot(gr, (row <= col).astype(F32), precision=HIGHEST, preferred_element_type=F32)
    ones_lane0 = (lax.broadcasted_iota(I32, (L, 128), 1) == 0).astype(BF16)

    for h in range(M_HEADS):
        bc = b_col[:, M_HEADS + h:M_HEADS + h + 1]
        br = b_row[M_HEADS + h:M_HEADS + h + 1, :]
        igr = gr[h:h + 1, :]
        m_prev = m_sc[h:h + 1, 0:1]
        dlog = jnp.where(causal, bc - br + igr, NEG)
        m_inter = bc + m_prev
        m_t = jnp.maximum(m_inter, jnp.max(dlog, axis=1, keepdims=True))
        w_intra = jnp.exp(dlog - m_t)
        w_inter = jnp.exp(m_inter - m_t)

        qh = (q_ref[:, h * M_DK:(h + 1) * M_DK] * (M_DK ** -0.5)).astype(BF16)
        kth = kt_ref[0, h * M_DK:(h + 1) * M_DK, :]
        v_ext = jnp.concatenate(
            [v_ref[:, h * M_DV:(h + 1) * M_DV].astype(BF16), ones_lane0], axis=1)
        s = jnp.dot(qh, kth.astype(BF16), preferred_element_type=F32) * w_intra
        c_h = c_sc[h]
        nd = (jnp.dot(s.astype(BF16), v_ext, preferred_element_type=F32)
              + w_inter * jnp.dot(qh, c_h.astype(BF16), preferred_element_type=F32))
        num = nd[:, :M_DV]
        den = nd[:, M_DV:M_DV + 1]
        hh = num / jnp.maximum(jnp.abs(den), jnp.exp(-m_t))
        hh = hh * lax.rsqrt(jnp.mean(hh * hh, axis=1, keepdims=True) + 1e-6)
        og = o_ref[:, h * M_DV:(h + 1) * M_DV]
        hn_ref[:, h * M_DV:(h + 1) * M_DV] = (
            hh * gain_ref[:, h * M_DV:(h + 1) * M_DV] * (1.0 / (1.0 + jnp.exp(-og))))

        m_new = m_t[L - 1:L, :]
        b_last = bc[L - 1:L, :]
        decay = jnp.exp(b_last + m_prev - m_new)
        w_s = jnp.exp(b_last - br + igr - m_new)
        kw = (kth * w_s).astype(BF16)
        c_sc[h] = decay * c_h + jnp.dot(kw, v_ext, preferred_element_type=F32)
        m_sc[h:h + 1, :] = jnp.broadcast_to(m_new, (1, 128))

    @pl.when(c == pl.num_programs(1) - 1)
    def _():
        cout_ref[0] = c_sc[...]
        mout_ref[0] = m_sc[...]


def _mlstm(proj, kt, gates, gates_t, c0_ext, m0_b, gain, nb, seq, chunk):
    nc = seq // chunk
    qk = M_HEADS * M_DK
    vd = M_HEADS * M_DV
    row_blk = lambda b, c: (b * nc + c, 0)
    return pl.pallas_call(
        _mlstm_kernel,
        grid=(nb, nc),
        in_specs=[pl.BlockSpec((chunk, qk), row_blk),
                  pl.BlockSpec((1, qk, chunk), lambda b, c: (b, 0, c)),
                  pl.BlockSpec((chunk, vd), lambda b, c: (b * nc + c, 1)),
                  pl.BlockSpec((chunk, vd), lambda b, c: (b * nc + c, 2)),
                  pl.BlockSpec((chunk, 128), row_blk),
                  pl.BlockSpec((1, 2 * M_HEADS, chunk), lambda b, c: (b, 0, c)),
                  pl.BlockSpec((1, M_HEADS, M_DK, 2 * M_DV), lambda b, c: (b, 0, 0, 0)),
                  pl.BlockSpec((1, M_HEADS, 128), lambda b, c: (b, 0, 0)),
                  pl.BlockSpec((1, vd), lambda b, c: (0, 0))],
        out_specs=[pl.BlockSpec((chunk, vd), row_blk),
                   pl.BlockSpec((1, M_HEADS, M_DK, 2 * M_DV), lambda b, c: (b, 0, 0, 0)),
                   pl.BlockSpec((1, M_HEADS, 128), lambda b, c: (b, 0, 0))],
        out_shape=[jax.ShapeDtypeStruct((nb * seq, vd), F32),
                   jax.ShapeDtypeStruct((nb, M_HEADS, M_DK, 2 * M_DV), F32),
                   jax.ShapeDtypeStruct((nb, M_HEADS, 128), F32)],
        scratch_shapes=[pltpu.VMEM((M_HEADS, M_DK, 2 * M_DV), F32),
                        pltpu.VMEM((M_HEADS, 128), F32)],
        compiler_params=_cparams(("parallel", "arbitrary")),
        name="mlstm",
    )(proj, kt, proj, proj, gates, gates_t, c0_ext, m0_b, gain.reshape(1, vd))


def _mlstm_layer(x_all, n_p, nb_p, seq_p, nb_s, seq_s, st_c, st_n, st_m, w_in, b_gate, gain):
    nt = x_all.shape[0]
    qk = M_HEADS * M_DK
    vd = M_HEADS * M_DV
    n_s = nb_s * seq_s
    proj = _matmul(x_all, w_in[:, :2 * qk + 2 * vd].astype(BF16))
    gates = _mlstm_gates(x_all, w_in[:, 2 * qk + 2 * vd:], b_gate)

    kt_p = proj[:n_p, qk:2 * qk].reshape(nb_p, seq_p, qk).transpose(0, 2, 1)
    gt_p = gates[:n_p, :2 * M_HEADS].reshape(nb_p, seq_p, 2 * M_HEADS).transpose(0, 2, 1)
    c0_p = jnp.zeros((nb_p, M_HEADS, M_DK, 2 * M_DV), F32)
    m0_p = jnp.zeros((nb_p, M_HEADS, 128), F32)
    hn_p, c_p, m_p = _mlstm(proj, kt_p, gates, gt_p, c0_p, m0_p, gain, nb_p, seq_p,
                            min(MLSTM_CHUNK, seq_p))

    ls = MLSTM_SAMPLE_CHUNK
    pad = ((0, 0), (0, ls - seq_s), (0, 0))
    proj_s = jnp.pad(proj[n_p:n_p + n_s].reshape(nb_s, seq_s, -1), pad)
    g_s = gates[n_p:n_p + n_s].reshape(nb_s, seq_s, 128)
    pad_row = jnp.where(jnp.arange(128) < M_HEADS, NEG, 0.0).astype(F32)
    g_s = jnp.concatenate([g_s, jnp.broadcast_to(pad_row, (nb_s, ls - seq_s, 128))], axis=1)
    kt_s = proj_s[:, :, qk:2 * qk].transpose(0, 2, 1)
    gt_s = g_s[:, :, :2 * M_HEADS].transpose(0, 2, 1)
    c0_s = jnp.concatenate([st_c, st_n[..., None],
                            jnp.zeros(st_c.shape[:3] + (M_DV - 1,), F32)], axis=-1)
    m0_s = jnp.broadcast_to(st_m[..., None], st_m.shape + (128,))
    hn_s, c_s, m_s = _mlstm(proj_s.reshape(nb_s * ls, -1), kt_s, g_s.reshape(nb_s * ls, 128), gt_s,
                            c0_s, m0_s, gain, nb_s, ls, ls)
    hn_s = hn_s.reshape(nb_s, ls, vd)[:, :seq_s].reshape(n_s, vd)
    hn = jnp.concatenate([hn_p, hn_s, jnp.zeros((nt - n_p - n_s, vd), F32)], axis=0)
    states = (c_p[..., :M_DV], c_p[..., M_DV], m_p[..., 0],
              c_s[..., :M_DV], c_s[..., M_DV], m_s[..., 0])
    return hn, states


def _rope_kernel(q_ref, k_ref, ca_ref, cb_ref, cc_ref, qo_ref, ko_ref, km_ref):
    ca, cb, cc = ca_ref[...], cb_ref[...], cc_ref[...]
    half = ROT_DIM // 2
    for j in range(q_ref.shape[1] // 128):
        sl = slice(j * 128, (j + 1) * 128)
        for src, dst in ((q_ref, qo_ref), (k_ref, ko_ref)):
            x = src[:, sl]
            dst[:, sl] = (x * ca + pltpu.roll(x, 128 - half, 1) * cb + pltpu.roll(x, half, 1) * cc)
    km_ref[0] = jnp.mean(ko_ref[...], axis=0, keepdims=True)


def _rope_tables(pos):
    half = ROT_DIM // 2
    inv = ROPE_THETA ** (-jnp.arange(0, ROT_DIM, 2, dtype=F32) / ROT_DIM)
    ang = pos.astype(F32)[:, None] * inv[None, :]
    cos, sin = jnp.cos(ang), jnp.sin(ang)
    lane = jnp.arange(128) % A_HD
    f = lane % half
    in_lo = lane < half
    in_hi = (lane >= half) & (lane < ROT_DIM)
    ca = jnp.where((in_lo | in_hi)[None, :], cos[:, f], 1.0)
    cb = jnp.where(in_lo[None, :], -sin[:, f], 0.0)
    cc = jnp.where(in_hi[None, :], sin[:, f], 0.0)
    return ca.astype(F32), cb.astype(F32), cc.astype(F32)


def _rope(qkv, tables, d):
    nt = qkv.shape[0]
    tm = MOBA_BLOCK
    spec_t = pl.BlockSpec((tm, 128), lambda i: (i, 0))
    return pl.pallas_call(
        _rope_kernel,
        grid=(nt // tm,),
        in_specs=[pl.BlockSpec((tm, d), lambda i: (i, 0)),
                  pl.BlockSpec((tm, d), lambda i: (i, 1)),
                  spec_t, spec_t, spec_t],
        out_specs=[pl.BlockSpec((tm, d), lambda i: (i, 0)),
                   pl.BlockSpec((tm, d), lambda i: (i, 0)),
                   pl.BlockSpec((1, 1, d), lambda i: (i, 0, 0))],
        out_shape=[jax.ShapeDtypeStruct((nt, d), F32),
                   jax.ShapeDtypeStruct((nt, d), F32),
                   jax.ShapeDtypeStruct((nt // tm, 1, d), F32)],
        compiler_params=_cparams(("parallel",)),
        name="rope",
    )(qkv, qkv, *tables)


def _moba_prompt_kernel(q_ref, k_ref, v_ref, km_ref, o_ref):
    qi = pl.program_id(2)
    blk = MOBA_BLOCK
    nblk = km_ref.shape[0]
    q = q_ref[...] * (A_HD ** -0.5)
    lane = lax.broadcasted_iota(I32, (blk, 128), 1)
    row = lax.broadcasted_iota(I32, (blk, blk), 0)
    col = lax.broadcasted_iota(I32, (blk, blk), 1)
    jidx = lax.broadcasted_iota(I32, (blk, nblk), 1)
    start = pl.multiple_of(qi * blk, blk)
    k_own = k_ref[pl.ds(start, blk), :].astype(BF16)
    v_own = v_ref[pl.ds(start, blk), :].astype(BF16)
    out = jnp.zeros((blk, 128), F32)

    for hh in range(2):
        hmask = (lane // A_HD) == hh
        qh = jnp.where(hmask, q, 0.0)
        qb = qh.astype(BF16)
        gate = lax.dot_general(qh, km_ref[...], NT_DIMS, precision=HIGHEST,
                               preferred_element_type=F32)
        valid = jidx < qi
        gm = jnp.where(valid, gate, NEG)
        rank = jnp.zeros((blk, nblk), I32)
        for jp in range(nblk - 1):
            gj = gm[:, jp:jp + 1]
            beats = (gj > gm) | ((gj == gm) & (jp < jidx))
            rank = rank + jnp.where(beats, 1, 0)
        sel = jnp.where(valid & (rank < MOBA_TOPK), 1.0, 0.0)

        s = lax.dot_general(qb, k_own, NT_DIMS, preferred_element_type=F32)
        s = jnp.where(col <= row, s, NEG)
        m0 = jnp.max(s, axis=1, keepdims=True)
        p = jnp.exp(s - m0)
        l0 = jnp.sum(p, axis=1, keepdims=True)
        a0 = jnp.dot(p.astype(BF16), v_own, preferred_element_type=F32)

        def body(j, carry):
            m, l, acc = carry
            off = pl.multiple_of(j * blk, blk)
            kj = k_ref[pl.ds(off, blk), :].astype(BF16)
            vj = v_ref[pl.ds(off, blk), :].astype(BF16)
            sj = lax.dot_general(qb, kj, NT_DIMS, preferred_element_type=F32)
            selj = jnp.sum(jnp.where(jidx == j, sel, 0.0), axis=1, keepdims=True)
            sj = jnp.where(selj > 0.0, sj, NEG)
            m_new = jnp.maximum(m, jnp.max(sj, axis=1, keepdims=True))
            alpha = jnp.exp(m - m_new)
            pj = jnp.exp(sj - m_new)
            l_new = alpha * l + jnp.sum(pj, axis=1, keepdims=True)
            acc_new = alpha * acc + jnp.dot(pj.astype(BF16), vj, preferred_element_type=F32)
            return m_new, l_new, acc_new

        _, l_f, acc_f = lax.fori_loop(0, qi, body, (m0, l0, a0))
        out = jnp.where(hmask, acc_f / l_f, out)
    o_ref[...] = out


def _moba_prompt(q_rot, k_rot, qkv, kmean, nb, seq, d):
    nq = seq // MOBA_BLOCK
    ng = d // 128
    return pl.pallas_call(
        _moba_prompt_kernel,
        grid=(nb, ng, nq),
        in_specs=[pl.BlockSpec((MOBA_BLOCK, 128), lambda b, g, i: (b * nq + i, g)),
                  pl.BlockSpec((seq, 128), lambda b, g, i: (b, g)),
                  pl.BlockSpec((seq, 128), lambda b, g, i: (b, 2 * ng + g)),
                  pl.BlockSpec((nq, 128), lambda b, g, i: (b, g))],
        out_specs=pl.BlockSpec((MOBA_BLOCK, 128), lambda b, g, i: (b * nq + i, g)),
        out_shape=jax.ShapeDtypeStruct((nb * seq, d), F32),
        compiler_params=_cparams(("parallel", "parallel", "arbitrary")),
        name="moba_prompt",
    )(q_rot, k_rot, qkv, kmean)


def _page_mean_kernel(pt_ref, page_ref, o_ref):
    r = pl.program_id(2)
    part = jnp.sum(page_ref[0], axis=0, keepdims=True) * (1.0 / MOBA_BLOCK)

    @pl.when(r == 0)
    def _():
        o_ref[0] = part

    @pl.when(r > 0)
    def _():
        o_ref[0] = o_ref[0] + part


def _cache_block_means(cache2d, page_table_flat, nb, n_blk, ppb, d):
    n_pages = n_blk * ppb
    grid_spec = pltpu.PrefetchScalarGridSpec(
        num_scalar_prefetch=1,
        grid=(nb, n_blk, ppb),
        in_specs=[pl.BlockSpec((1, PAGE_SIZE, d),
                               lambda b, j, r, pt: (pt[b * n_pages + j * ppb + r], 0, 0))],
        out_specs=pl.BlockSpec((1, 1, d), lambda b, j, r, pt: (b * n_blk + j, 0, 0)),
    )
    return pl.pallas_call(
        _page_mean_kernel,
        grid_spec=grid_spec,
        out_shape=jax.ShapeDtypeStruct((nb * n_blk, 1, d), F32),
        compiler_params=_cparams(("parallel", "parallel", "arbitrary")),
        name="cache_block_means",
    )(page_table_flat, cache2d)


def _expand_heads(q, t_len):
    d = q.shape[1]
    rows = t_len * A_HEADS
    rep = jnp.concatenate([jnp.broadcast_to(q[t:t + 1, :], (A_HEADS, d)) for t in range(t_len)], axis=0)
    r = lax.broadcasted_iota(I32, (rows, d), 0)
    c = lax.broadcasted_iota(I32, (rows, d), 1)
    return jnp.where((c // A_HD) == (r % A_HEADS), rep, 0.0)


def _sample_select_kernel(q_ref, km_ref, sel_ref):
    t_len = q_ref.shape[1]
    n_blk = km_ref.shape[1]
    qx = _expand_heads(q_ref[0], t_len)
    gate = lax.dot_general(qx, km_ref[0], NT_DIMS, precision=HIGHEST,
                           preferred_element_type=F32)
    jidx = lax.broadcasted_iota(I32, gate.shape, 1)
    sel = jnp.zeros(gate.shape, F32)
    for _ in range(min(MOBA_TOPK, n_blk)):
        m = jnp.max(gate, axis=1, keepdims=True)
        first = jnp.min(jnp.where(gate == m, jidx, n_blk), axis=1, keepdims=True)
        hit = jidx == first
        sel = jnp.where(hit, 1.0, sel)
        gate = jnp.where(hit, -jnp.inf, gate)
    sel_ref[0] = sel


def _sample_select(q_s, kmean_s):
    nb, t_len, d = q_s.shape
    n_blk = kmean_s.shape[1]
    rows = t_len * A_HEADS
    return pl.pallas_call(
        _sample_select_kernel,
        grid=(nb,),
        in_specs=[pl.BlockSpec((1, t_len, d), lambda b: (b, 0, 0)),
                  pl.BlockSpec((1, n_blk, d), lambda b: (b, 0, 0))],
        out_specs=pl.BlockSpec((1, rows, n_blk), lambda b: (b, 0, 0)),
        out_shape=jax.ShapeDtypeStruct((nb, rows, n_blk), F32),
        compiler_params=_cparams(("parallel",)),
        name="sample_select",
    )(q_s, kmean_s)


def _moba_sample_kernel(pt_ref, q_ref, kn_ref, vn_ref, sel_ref, kp_ref, vp_ref, o_ref,
                        qx_sc, m_sc, l_sc, acc_sc, *, ppb):
    p = pl.program_id(1)
    t_len = q_ref.shape[1]
    d = q_ref.shape[2]
    rows = t_len * A_HEADS
    n_blk = sel_ref.shape[2]

    @pl.when(p == 0)
    def _():
        qx = _expand_heads(q_ref[0] * (A_HD ** -0.5), t_len)
        qx_sc[...] = qx
        s = lax.dot_general(qx.astype(BF16), kn_ref[0].astype(BF16), NT_DIMS,
                            preferred_element_type=F32)
        r = lax.broadcasted_iota(I32, s.shape, 0)
        c = lax.broadcasted_iota(I32, s.shape, 1)
        s = jnp.where(c <= r // A_HEADS, s, NEG)
        m0 = jnp.max(s, axis=1, keepdims=True)
        pw = jnp.exp(s - m0)
        m_sc[...] = m0
        l_sc[...] = jnp.sum(pw, axis=1, keepdims=True)
        acc_sc[...] = jnp.dot(pw.astype(BF16), vn_ref[0].astype(BF16), preferred_element_type=F32)

    qx = qx_sc[...].astype(BF16)
    s = lax.dot_general(qx, kp_ref[0].astype(BF16), NT_DIMS, preferred_element_type=F32)
    jidx = lax.broadcasted_iota(I32, (rows, n_blk), 1)
    selj = jnp.sum(jnp.where(jidx == p // ppb, sel_ref[0], 0.0), axis=1, keepdims=True)
    s = jnp.where(selj > 0.0, s, NEG)
    m_old = m_sc[...]
    m_new = jnp.maximum(m_old, jnp.max(s, axis=1, keepdims=True))
    alpha = jnp.exp(m_old - m_new)
    pw = jnp.exp(s - m_new)
    m_sc[...] = m_new
    l_sc[...] = alpha * l_sc[...] + jnp.sum(pw, axis=1, keepdims=True)
    acc_sc[...] = alpha * acc_sc[...] + jnp.dot(pw.astype(BF16), vp_ref[0].astype(BF16),
                                                preferred_element_type=F32)

    @pl.when(p == pl.num_programs(1) - 1)
    def _():
        res = acc_sc[...] / l_sc[...]
        r = lax.broadcasted_iota(I32, (rows, d), 0)
        c = lax.broadcasted_iota(I32, (rows, d), 1)
        res = jnp.where((c // A_HD) == (r % A_HEADS), res, 0.0)
        o_ref[0] = jnp.concatenate(
            [jnp.sum(res[t * A_HEADS:(t + 1) * A_HEADS], axis=0, keepdims=True) for t in range(t_len)],
            axis=0)


def _moba_sample(q_s, k_s, v_s, sel, cache_k2d, cache_v2d, page_table_flat, ppb):
    nb, t_len, d = q_s.shape
    rows = t_len * A_HEADS
    n_blk = sel.shape[2]
    n_pages = n_blk * ppb
    page_map = lambda b, p, pt: (pt[b * n_pages + p], 0, 0)
    seq_map = lambda b, p, pt: (b, 0, 0)
    t_pad = 16
    k_s = jnp.pad(k_s, ((0, 0), (0, t_pad - t_len), (0, 0)))
    v_s = jnp.pad(v_s, ((0, 0), (0, t_pad - t_len), (0, 0)))
    grid_spec = pltpu.PrefetchScalarGridSpec(
        num_scalar_prefetch=1,
        grid=(nb, n_pages),
        in_specs=[pl.BlockSpec((1, t_len, d), seq_map),
                  pl.BlockSpec((1, t_pad, d), seq_map),
                  pl.BlockSpec((1, t_pad, d), seq_map),
                  pl.BlockSpec((1, rows, n_blk), seq_map),
                  pl.BlockSpec((1, PAGE_SIZE, d), page_map),
                  pl.BlockSpec((1, PAGE_SIZE, d), page_map)],
        out_specs=pl.BlockSpec((1, t_len, d), seq_map),
        scratch_shapes=[pltpu.VMEM((rows, d), F32),
                        pltpu.VMEM((rows, 1), F32),
                        pltpu.VMEM((rows, 1), F32),
                        pltpu.VMEM((rows, d), F32)],
    )
    return pl.pallas_call(
        functools.partial(_moba_sample_kernel, ppb=ppb),
        grid_spec=grid_spec,
        out_shape=jax.ShapeDtypeStruct((nb, t_len, d), F32),
        compiler_params=_cparams(("parallel", "arbitrary")),
        name="moba_sample",
    )(page_table_flat, q_s, k_s, v_s, sel, cache_k2d, cache_v2d)


def _top16_desc(vals, payload, n_rows):
    t = vals.shape[1]
    r16 = lax.broadcasted_iota(I32, (P_TOPK, t), 0)
    out_v = jnp.zeros((P_TOPK, t), F32)
    out_p = jnp.zeros((P_TOPK, t), I32)
    big = jnp.int32(2 ** 30)
    for r in range(P_TOPK):
        m = jnp.max(vals, axis=0, keepdims=True)
        pick = jnp.min(jnp.where(vals == m, payload, big), axis=0, keepdims=True)
        vals = jnp.where(payload == pick, -jnp.inf, vals)
        out_v = jnp.where(r16 == r, m, out_v)
        out_p = jnp.where(r16 == r, pick, out_p)
    return out_v, out_p


def _peer_topk_kernel(q_ref, keys_ref, idx_ref, g_ref):
    t = q_ref.shape[0]
    kio = lax.broadcasted_iota(I32, (P_NKEYS, t), 0)
    e_heads, g_heads = [], []
    for h in range(P_HEADS):
        sv, si = [], []
        for p in range(2):
            c0 = (h * 2 + p) * (P_NKEYS)
            scores = lax.dot_general(keys_ref[h, p], q_ref[:, c0:c0 + 128], NT_DIMS,
                                     precision=HIGHEST, preferred_element_type=F32)
            v, i = _top16_desc(scores, kio, P_NKEYS)
            sv.append(v)
            si.append(i)
        cand = jnp.concatenate([sv[0][i:i + 1, :] + sv[1] for i in range(P_TOPK)], axis=0)
        eid = jnp.concatenate([si[0][i:i + 1, :] * P_NKEYS + si[1] for i in range(P_TOPK)], axis=0)
        cv, ce = _top16_desc(cand, eid, P_TOPK * P_TOPK)
        ex = jnp.exp(cv - cv[0:1, :])
        g_heads.append(ex / jnp.sum(ex, axis=0, keepdims=True))
        e_heads.append(ce)
    idx_ref[...] = jnp.concatenate(e_heads, axis=0).T
    g_ref[...] = jnp.concatenate(g_heads, axis=0).T


def _peer_topk(q, keys):
    nt = q.shape[0]
    t = TOK_BLOCK
    return pl.pallas_call(
        _peer_topk_kernel,
        grid=(nt // t,),
        in_specs=[pl.BlockSpec((t, q.shape[1]), lambda i: (i, 0)),
                  pl.BlockSpec(keys.shape, lambda i: (0, 0, 0, 0))],
        out_specs=[pl.BlockSpec((t, P_SLOTS), lambda i: (i, 0)),
                   pl.BlockSpec((t, P_SLOTS), lambda i: (i, 0))],
        out_shape=[jax.ShapeDtypeStruct((nt, P_SLOTS), I32),
                   jax.ShapeDtypeStruct((nt, P_SLOTS), F32)],
        compiler_params=_cparams(("parallel",)),
        name="peer_topk",
    )(q, keys)


def _gather_row(tab_ref, e):
    pair = tab_ref[e >> 1].astype(F32)
    return jnp.where((e & 1) == 1, pair[8:16, :], pair[0:8, :])


def _peer_u_kernel(idx_ref, xf_ref, g_ref, tab_ref, sel_ref, w_ref, p_sc, hid_sc):
    tb = g_ref.shape[0]

    def token(n, carry):
        xt = xf_ref[pl.ds(pl.multiple_of(n * 8, 8), 8), :]
        for k in range(P_SLOTS):
            p_sc[k * 8:(k + 1) * 8, :] = _gather_row(tab_ref, idx_ref[n, k]) * xt
        sub = lax.broadcasted_iota(I32, (8, 128), 0)
        onehot = (sub == (n % 8)).astype(BF16)
        part = lax.dot_general(onehot, p_sc[...].astype(BF16), NT_DIMS,
                               preferred_element_type=F32)
        hi = part.astype(BF16)
        lo = (part - hi.astype(F32)).astype(BF16)
        hid8 = (jnp.dot(hi, sel_ref[...], preferred_element_type=F32)
                + jnp.dot(lo, sel_ref[...], preferred_element_type=F32))
        base = pl.multiple_of((n // 8) * 8, 8)
        hid_sc[pl.ds(base, 8), :] = hid_sc[pl.ds(base, 8), :] + hid8
        return carry

    hid_sc[...] = jnp.zeros_like(hid_sc)
    lax.fori_loop(0, tb, token, 0)
    hid = hid_sc[...]
    gelu = 0.5 * hid * (1.0 + lax.erf(hid * (2.0 ** -0.5)))
    w_ref[...] = g_ref[...] * gelu


def _peer_v_kernel(idx_ref, w_ref, tab_ref, o_ref):
    tb = idx_ref.shape[0]

    def token(n, carry):
        accs = [jnp.zeros((8, 128), F32) for _ in range(4)]
        for k in range(P_SLOTS):
            accs[k % 4] = accs[k % 4] + w_ref[n, k] * _gather_row(tab_ref, idx_ref[n, k])
        o_ref[pl.ds(pl.multiple_of(n * 8, 8), 8), :] = (accs[0] + accs[1]) + (accs[2] + accs[3])
        return carry

    lax.fori_loop(0, tb, token, 0)


def _peer_gather(x, idx, g, u_pairs, v_pairs):
    nt, d = x.shape
    fold = d // 128
    tb = TOK_BLOCK
    n_pair = u_pairs.shape[0]
    tab_spec = pl.BlockSpec((n_pair, 16, 128), lambda i: (0, 0, 0), pipeline_mode=pl.Buffered(1))
    smem_spec = pl.BlockSpec((tb, P_SLOTS), lambda i: (i, 0), memory_space=pltpu.SMEM)
    sel = (jnp.arange(P_SLOTS * fold)[:, None] // fold == jnp.arange(128)[None, :]).astype(BF16)
    w = pl.pallas_call(
        _peer_u_kernel,
        grid=(nt // tb,),
        in_specs=[smem_spec,
                  pl.BlockSpec((tb * fold, 128), lambda i: (i, 0)),
                  pl.BlockSpec((tb, P_SLOTS), lambda i: (i, 0)),
                  tab_spec,
                  pl.BlockSpec((P_SLOTS * fold, 128), lambda i: (0, 0))],
        out_specs=pl.BlockSpec((tb, P_SLOTS), lambda i: (i, 0)),
        out_shape=jax.ShapeDtypeStruct((nt, P_SLOTS), F32),
        scratch_shapes=[pltpu.VMEM((P_SLOTS * fold, 128), F32),
                        pltpu.VMEM((tb, P_SLOTS), F32)],
        compiler_params=_cparams(("parallel",)),
        name="peer_u",
    )(idx, x.reshape(nt * fold, 128), g, u_pairs, sel)
    out = pl.pallas_call(
        _peer_v_kernel,
        grid=(nt // tb,),
        in_specs=[smem_spec, smem_spec, tab_spec],
        out_specs=pl.BlockSpec((tb * fold, 128), lambda i: (i, 0)),
        out_shape=jax.ShapeDtypeStruct((nt * fold, 128), F32),
        compiler_params=_cparams(("parallel",)),
        name="peer_v",
    )(idx, w, v_pairs)
    return out.reshape(nt, d)


def _pair_table(tab):
    e, d = tab.shape
    return tab.astype(BF16).reshape(e // 2, 2 * (d // 128), 128)


def _peer_layer(x, w_q, keys, u_tab, v_tab, ln_g, ln_b):
    q = _matmul(x, w_q.astype(BF16))
    idx, g = _peer_topk(q, keys)
    y = _peer_gather(x, idx, g, _pair_table(u_tab), _pair_table(v_tab))
    return _res_ln(x, y, ln_g, ln_b)


def kernel(x_prompt, x_sample, state_mlstm_C, state_mlstm_n, state_mlstm_m, cache_k, cache_v, page_table,
           w_in_a, b_gate_a, mh_norm_a, w_out_a, w_kv, w_q_b, w_out_b,
           ln_mix_g, ln_mix_b, ln_ffn_g, ln_ffn_b, peer_wq, peer_keys, peer_u, peer_v):
    nb_p, seq_p, d = x_prompt.shape
    nb_s, seq_s, _ = x_sample.shape
    n_p, n_s = nb_p * seq_p, nb_s * seq_s
    nt = -(-(n_p + n_s) // ROW_TILE) * ROW_TILE
    past_len = page_table.shape[1] * PAGE_SIZE
    ppb = MOBA_BLOCK // PAGE_SIZE
    n_past_blk = past_len // MOBA_BLOCK
    assert past_len % MOBA_BLOCK == 0 and seq_p % MOBA_BLOCK == 0 and n_p % ROW_TILE == 0
    assert state_mlstm_C.shape[0] == 1 and w_q_b.shape[0] == 1

    x = jnp.concatenate([x_prompt.reshape(n_p, d), x_sample.reshape(n_s, d),
                         jnp.zeros((nt - n_p - n_s, d), F32)], axis=0)

    hn, states = _mlstm_layer(x, n_p, nb_p, seq_p, nb_s, seq_s,
                              state_mlstm_C[0], state_mlstm_n[0], state_mlstm_m[0],
                              w_in_a[0], b_gate_a[0], mh_norm_a[0])
    x = _matmul_res_ln(hn, w_out_a[0].astype(BF16), x, ln_mix_g[0], ln_mix_b[0])
    x = _peer_layer(x, peer_wq[0], peer_keys[0], peer_u[0], peer_v[0], ln_ffn_g[0], ln_ffn_b[0])

    qkv = _matmul(x, jnp.concatenate([w_q_b[0], w_kv], axis=1).astype(BF16))
    pos = jnp.concatenate([jnp.tile(jnp.arange(seq_p), nb_p),
                           past_len + jnp.tile(jnp.arange(seq_s), nb_s),
                           jnp.zeros((nt - n_p - n_s,), I32)])
    q_rot, k_rot, kmean = _rope(qkv, _rope_tables(pos), d)
    attn_p = _moba_prompt(q_rot, k_rot, qkv, kmean.reshape(nt // MOBA_BLOCK, d), nb_p, seq_p, d)

    q_s = q_rot[n_p:n_p + n_s].reshape(nb_s, seq_s, d)
    k_s = k_rot[n_p:n_p + n_s].reshape(nb_s, seq_s, d)
    v_s = qkv[n_p:n_p + n_s, 2 * d:].reshape(nb_s, seq_s, d)
    cache_k2d = cache_k.reshape(cache_k.shape[0], PAGE_SIZE, d)
    cache_v2d = cache_v.reshape(cache_v.shape[0], PAGE_SIZE, d)
    pt_flat = page_table.reshape(-1).astype(I32)
    kmean_s = _cache_block_means(cache_k2d, pt_flat, nb_s, n_past_blk, ppb, d)
    sel = _sample_select(q_s, kmean_s.reshape(nb_s, n_past_blk, d))
    attn_s = _moba_sample(q_s, k_s, v_s, sel, cache_k2d, cache_v2d, pt_flat, ppb)

    attn = jnp.concatenate([attn_p, attn_s.reshape(n_s, d), jnp.zeros((nt - n_p - n_s, d), F32)], axis=0)
    x = _matmul_res_ln(attn, w_out_b[0].astype(BF16), x, ln_mix_g[1], ln_mix_b[1])
    x = _peer_layer(x, peer_wq[1], peer_keys[1], peer_u[1], peer_v[1], ln_ffn_g[1], ln_ffn_b[1])

    y_prompt = x[:n_p].reshape(nb_p, seq_p, d)
    y_sample = x[n_p:n_p + n_s].reshape(nb_s, seq_s, d)
    c_p, n_pr, m_p, c_s, n_sm, m_s = states
    k_prompt = k_rot[:n_p].reshape(nb_p, seq_p, A_HEADS, A_HD)
    v_prompt = qkv[:n_p, 2 * d:].reshape(nb_p, seq_p, A_HEADS, A_HD)
    return (y_prompt, y_sample, c_p[None], n_pr[None], m_p[None], c_s[None], n_sm[None], m_s[None],
            k_prompt, v_prompt, k_s.reshape(nb_s, seq_s, A_HEADS, A_HD), v_s.reshape(nb_s, seq_s, A_HEADS, A_HD))
```

```python
import functools

import jax
import jax.numpy as jnp
from jax import lax
from jax.experimental import pallas as pl
from jax.experimental.pallas import tpu as pltpu

F32 = jnp.float32
BF16 = jnp.bfloat16
I32 = jnp.int32
HIGHEST = lax.Precision.HIGHEST

DEPTH = 2
DN_ALPHA = (2.0 * DEPTH) ** 0.25
LN_EPS = 1e-5

M_HEADS = 8
M_DK = 64
M_DV = 128
MLSTM_CHUNK = 256
MLSTM_SAMPLE_CHUNK = 128

A_HEADS = 16
A_HD = 64
ROT_DIM = 16
ROPE_THETA = 500000.0
MOBA_BLOCK = 256
MOBA_TOPK = 3
PAGE_SIZE = 128

P_HEADS = 8
P_NKEYS = 128
P_TOPK = 16
P_SLOTS = P_HEADS * P_TOPK

ROW_TILE = 512
TOK_BLOCK = 128
NEG = -1e30
VMEM_LIMIT = 56 * 1024 * 1024

NT_DIMS = (((1,), (1,)), ((), ()))


def _cparams(semantics, vmem=VMEM_LIMIT):
    return pltpu.CompilerParams(dimension_semantics=semantics, vmem_limit_bytes=vmem)


def _mm_kernel(x_ref, w_ref, o_ref):
    o_ref[...] = jnp.dot(x_ref[...].astype(BF16), w_ref[...], preferred_element_type=F32)


def _matmul(x, w_bf16, tm=ROW_TILE):
    m, k = x.shape
    n = w_bf16.shape[1]
    return pl.pallas_call(
        _mm_kernel,
        grid=(m // tm,),
        in_specs=[pl.BlockSpec((tm, k), lambda i: (i, 0)),
                  pl.BlockSpec((k, n), lambda i: (0, 0))],
        out_specs=pl.BlockSpec((tm, n), lambda i: (i, 0)),
        out_shape=jax.ShapeDtypeStruct((m, n), F32),
        compiler_params=_cparams(("parallel",)),
        name="matmul",
    )(x, w_bf16)


def _layer_norm_rows(z, g, b):
    mu = jnp.mean(z, axis=-1, keepdims=True)
    zc = z - mu
    var = jnp.mean(zc * zc, axis=-1, keepdims=True)
    return zc * lax.rsqrt(var + LN_EPS) * g + b


def _mm_ln_kernel(a_ref, w_ref, x_ref, g_ref, b_ref, o_ref):
    y = jnp.dot(a_ref[...].astype(BF16), w_ref[...], preferred_element_type=F32)
    o_ref[...] = _layer_norm_rows(DN_ALPHA * x_ref[...] + y, g_ref[...], b_ref[...])


def _matmul_res_ln(a, w_bf16, x_res, g, b, tm=ROW_TILE):
    m, k = a.shape
    n = w_bf16.shape[1]
    return pl.pallas_call(
        _mm_ln_kernel,
        grid=(m // tm,),
        in_specs=[pl.BlockSpec((tm, k), lambda i: (i, 0)),
                  pl.BlockSpec((k, n), lambda i: (0, 0)),
                  pl.BlockSpec((tm, n), lambda i: (i, 0)),
                  pl.BlockSpec((1, n), lambda i: (0, 0)),
                  pl.BlockSpec((1, n), lambda i: (0, 0))],
        out_specs=pl.BlockSpec((tm, n), lambda i: (i, 0)),
        out_shape=jax.ShapeDtypeStruct((m, n), F32),
        compiler_params=_cparams(("parallel",)),
        name="matmul_res_ln",
    )(a, w_bf16, x_res, g.reshape(1, n), b.reshape(1, n))


def _res_ln_kernel(x_ref, y_ref, g_ref, b_ref, o_ref):
    o_ref[...] = _layer_norm_rows(DN_ALPHA * x_ref[...] + y_ref[...], g_ref[...], b_ref[...])


def _res_ln(x, y, g, b, tm=ROW_TILE):
    m, n = x.shape
    return pl.pallas_call(
        _res_ln_kernel,
        grid=(m // tm,),
        in_specs=[pl.BlockSpec((tm, n), lambda i: (i, 0)),
                  pl.BlockSpec((tm, n), lambda i: (i, 0)),
                  pl.BlockSpec((1, n), lambda i: (0, 0)),
                  pl.BlockSpec((1, n), lambda i: (0, 0))],
        out_specs=pl.BlockSpec((tm, n), lambda i: (i, 0)),
        out_shape=jax.ShapeDtypeStruct((m, n), F32),
        compiler_params=_cparams(("parallel",)),
        name="res_ln",
    )(x, y, g.reshape(1, n), b.reshape(1, n))


def _gates_kernel(x_ref, w_ref, b_ref, o_ref):
    g = jnp.dot(x_ref[...], w_ref[...], precision=HIGHEST, preferred_element_type=F32) + b_ref[...]
    lane = lax.broadcasted_iota(I32, g.shape, 1)
    log_sig = jnp.minimum(g, 0.0) - jnp.log1p(jnp.exp(-jnp.abs(g)))
    o_ref[...] = jnp.where(lane < M_HEADS, g, log_sig)


def _mlstm_gates(x, w_gate, b_gate, tm=ROW_TILE):
    m, k = x.shape
    w = jnp.zeros((k, 128), F32).at[:, :2 * M_HEADS].set(w_gate)
    b = jnp.zeros((1, 128), F32).at[0, :2 * M_HEADS].set(b_gate)
    return pl.pallas_call(
        _gates_kernel,
        grid=(m // tm,),
        in_specs=[pl.BlockSpec((tm, k), lambda i: (i, 0)),
                  pl.BlockSpec((k, 128), lambda i: (0, 0)),
                  pl.BlockSpec((1, 128), lambda i: (0, 0))],
        out_specs=pl.BlockSpec((tm, 128), lambda i: (i, 0)),
        out_shape=jax.ShapeDtypeStruct((m, 128), F32),
        compiler_params=_cparams(("parallel",)),
        name="mlstm_gates",
    )(x, w, b)


def _mlstm_kernel(q_ref, kt_ref, v_ref, o_ref, gc_ref, gr_ref, c0_ref, m0_ref, gain_ref,
                  hn_ref, cout_ref, mout_ref, c_sc, m_sc):
    c = pl.program_id(1)
    L = q_ref.shape[0]

    @pl.when(c == 0)
    def _():
        c_sc[...] = c0_ref[0]
        m_sc[...] = m0_ref[0]

    gc = gc_ref[...]
    gr = gr_ref[0]
    row = lax.broadcasted_iota(I32, (L, L), 0)
    col = lax.broadcasted_iota(I32, (L, L), 1)
    causal = col <= row
    b_col = jnp.dot(causal.astype(F32), gc, precision=HIGHEST, preferred_element_type=F32)
    b_row = jnp.dot(gr, (row <= col).astype(F32), precision=HIGHEST, preferred_element_type=F32)
    ones_lane0 = (lax.broadcasted_iota(I32, (L, 128), 1) == 0).astype(BF16)

    for h in range(M_HEADS):
        bc = b_col[:, M_HEADS + h:M_HEADS + h + 1]
        br = b_row[M_HEADS + h:M_HEADS + h + 1, :]
        igr = gr[h:h + 1, :]
        m_prev = m_sc[h:h + 1, 0:1]
        dlog = jnp.where(causal, bc - br + igr, NEG)
        m_inter = bc + m_prev
        m_t = jnp.maximum(m_inter, jnp.max(dlog, axis=1, keepdims=True))
        w_intra = jnp.exp(dlog - m_t)
        w_inter = jnp.exp(m_inter - m_t)

        qh = (q_ref[:, h * M_DK:(h + 1) * M_DK] * (M_DK ** -0.5)).astype(BF16)
        kth = kt_ref[0, h * M_DK:(h + 1) * M_DK, :]
        v_ext = jnp.concatenate(
            [v_ref[:, h * M_DV:(h + 1) * M_DV].astype(BF16), ones_lane0], axis=1)
        s = jnp.dot(qh, kth.astype(BF16), preferred_element_type=F32) * w_intra
        c_h = c_sc[h]
        nd = (jnp.dot(s.astype(BF16), v_ext, preferred_element_type=F32)
              + w_inter * jnp.dot(qh, c_h.astype(BF16), preferred_element_type=F32))
        num = nd[:, :M_DV]
        den = nd[:, M_DV:M_DV + 1]
        hh = num / jnp.maximum(jnp.abs(den), jnp.exp(-m_t))
        hh = hh * lax.rsqrt(jnp.mean(hh * hh, axis=1, keepdims=True) + 1e-6)
        og = o_ref[:, h * M_DV:(h + 1) * M_DV]
        hn_ref[:, h * M_DV:(h + 1) * M_DV] = (
            hh * gain_ref[:, h * M_DV:(h + 1) * M_DV] * (1.0 / (1.0 + jnp.exp(-og))))

        m_new = m_t[L - 1:L, :]
        b_last = bc[L - 1:L, :]
        decay = jnp.exp(b_last + m_prev - m_new)
        w_s = jnp.exp(b_last - br + igr - m_new)
        kw = (kth * w_s).astype(BF16)
        c_sc[h] = decay * c_h + jnp.dot(kw, v_ext, preferred_element_type=F32)
        m_sc[h:h + 1, :] = jnp.broadcast_to(m_new, (1, 128))

    @pl.when(c == pl.num_programs(1) - 1)
    def _():
        cout_ref[0] = c_sc[...]
        mout_ref[0] = m_sc[...]


def _mlstm(proj, kt, gates, gates_t, c0_ext, m0_b, gain, nb, seq, chunk):
    nc = seq // chunk
    qk = M_HEADS * M_DK
    vd = M_HEADS * M_DV
    row_blk = lambda b, c: (b * nc + c, 0)
    return pl.pallas_call(
        _mlstm_kernel,
        grid=(nb, nc),
        in_specs=[pl.BlockSpec((chunk, qk), row_blk),
                  pl.BlockSpec((1, qk, chunk), lambda b, c: (b, 0, c)),
                  pl.BlockSpec((chunk, vd), lambda b, c: (b * nc + c, 1)),
                  pl.BlockSpec((chunk, vd), lambda b, c: (b * nc + c, 2)),
                  pl.BlockSpec((chunk, 128), row_blk),
                  pl.BlockSpec((1, 2 * M_HEADS, chunk), lambda b, c: (b, 0, c)),
                  pl.BlockSpec((1, M_HEADS, M_DK, 2 * M_DV), lambda b, c: (b, 0, 0, 0)),
                  pl.BlockSpec((1, M_HEADS, 128), lambda b, c: (b, 0, 0)),
                  pl.BlockSpec((1, vd), lambda b, c: (0, 0))],
        out_specs=[pl.BlockSpec((chunk, vd), row_blk),
                   pl.BlockSpec((1, M_HEADS, M_DK, 2 * M_DV), lambda b, c: (b, 0, 0, 0)),
                   pl.BlockSpec((1, M_HEADS, 128), lambda b, c: (b, 0, 0))],
        out_shape=[jax.ShapeDtypeStruct((nb * seq, vd), F32),
                   jax.ShapeDtypeStruct((nb, M_HEADS, M_DK, 2 * M_DV), F32),
                   jax.ShapeDtypeStruct((nb, M_HEADS, 128), F32)],
        scratch_shapes=[pltpu.VMEM((M_HEADS, M_DK, 2 * M_DV), F32),
                        pltpu.VMEM((M_HEADS, 128), F32)],
        compiler_params=_cparams(("parallel", "arbitrary")),
        name="mlstm",
    )(proj, kt, proj, proj, gates, gates_t, c0_ext, m0_b, gain.reshape(1, vd))


def _mlstm_layer(x_all, n_p, nb_p, seq_p, nb_s, seq_s, st_c, st_n, st_m, w_in, b_gate, gain):
    nt = x_all.shape[0]
    qk = M_HEADS * M_DK
    vd = M_HEADS * M_DV
    n_s = nb_s * seq_s
    proj = _matmul(x_all, w_in[:, :2 * qk + 2 * vd].astype(BF16))
    gates = _mlstm_gates(x_all, w_in[:, 2 * qk + 2 * vd:], b_gate)

    kt_p = proj[:n_p, qk:2 * qk].reshape(nb_p, seq_p, qk).transpose(0, 2, 1)
    gt_p = gates[:n_p, :2 * M_HEADS].reshape(nb_p, seq_p, 2 * M_HEADS).transpose(0, 2, 1)
    c0_p = jnp.zeros((nb_p, M_HEADS, M_DK, 2 * M_DV), F32)
    m0_p = jnp.zeros((nb_p, M_HEADS, 128), F32)
    hn_p, c_p, m_p = _mlstm(proj, kt_p, gates, gt_p, c0_p, m0_p, gain, nb_p, seq_p,
                            min(MLSTM_CHUNK, seq_p))

    ls = MLSTM_SAMPLE_CHUNK
    pad = ((0, 0), (0, ls - seq_s), (0, 0))
    proj_s = jnp.pad(proj[n_p:n_p + n_s].reshape(nb_s, seq_s, -1), pad)
    g_s = gates[n_p:n_p + n_s].reshape(nb_s, seq_s, 128)
    pad_row = jnp.where(jnp.arange(128) < M_HEADS, NEG, 0.0).astype(F32)
    g_s = jnp.concatenate([g_s, jnp.broadcast_to(pad_row, (nb_s, ls - seq_s, 128))], axis=1)
    kt_s = proj_s[:, :, qk:2 * qk].transpose(0, 2, 1)
    gt_s = g_s[:, :, :2 * M_HEADS].transpose(0, 2, 1)
    c0_s = jnp.concatenate([st_c, st_n[..., None],
                            jnp.zeros(st_c.shape[:3] + (M_DV - 1,), F32)], axis=-1)
    m0_s = jnp.broadcast_to(st_m[..., None], st_m.shape + (128,))
    hn_s, c_s, m_s = _mlstm(proj_s.reshape(nb_s * ls, -1), kt_s, g_s.reshape(nb_s * ls, 128), gt_s,
                            c0_s, m0_s, gain, nb_s, ls, ls)
    hn_s = hn_s.reshape(nb_s, ls, vd)[:, :seq_s].reshape(n_s, vd)
    hn = jnp.concatenate([hn_p, hn_s, jnp.zeros((nt - n_p - n_s, vd), F32)], axis=0)
    states = (c_p[..., :M_DV], c_p[..., M_DV], m_p[..., 0],
              c_s[..., :M_DV], c_s[..., M_DV], m_s[..., 0])
    return hn, states


def _rope_kernel(q_ref, k_ref, ca_ref, cb_ref, cc_ref, qo_ref, ko_ref, km_ref):
    ca, cb, cc = ca_ref[...], cb_ref[...], cc_ref[...]
    half = ROT_DIM // 2
    for j in range(q_ref.shape[1] // 128):
        sl = slice(j * 128, (j + 1) * 128)
        for src, dst in ((q_ref, qo_ref), (k_ref, ko_ref)):
            x = src[:, sl]
            dst[:, sl] = (x * ca + pltpu.roll(x, 128 - half, 1) * cb + pltpu.roll(x, half, 1) * cc)
    km_ref[0] = jnp.mean(ko_ref[...], axis=0, keepdims=True)


def _rope_tables(pos):
    half = ROT_DIM // 2
    inv = ROPE_THETA ** (-jnp.arange(0, ROT_DIM, 2, dtype=F32) / ROT_DIM)
    ang = pos.astype(F32)[:, None] * inv[None, :]
    cos, sin = jnp.cos(ang), jnp.sin(ang)
    lane = jnp.arange(128) % A_HD
    f = lane % half
    in_lo = lane < half
    in_hi = (lane >= half) & (lane < ROT_DIM)
    ca = jnp.where((in_lo | in_hi)[None, :], cos[:, f], 1.0)
    cb = jnp.where(in_lo[None, :], -sin[:, f], 0.0)
    cc = jnp.where(in_hi[None, :], sin[:, f], 0.0)
    return ca.astype(F32), cb.astype(F32), cc.astype(F32)


def _rope(qkv, tables, d):
    nt = qkv.shape[0]
    tm = MOBA_BLOCK
    spec_t = pl.BlockSpec((tm, 128), lambda i: (i, 0))
    return pl.pallas_call(
        _rope_kernel,
        grid=(nt // tm,),
        in_specs=[pl.BlockSpec((tm, d), lambda i: (i, 0)),
                  pl.BlockSpec((tm, d), lambda i: (i, 1)),
                  spec_t, spec_t, spec_t],
        out_specs=[pl.BlockSpec((tm, d), lambda i: (i, 0)),
                   pl.BlockSpec((tm, d), lambda i: (i, 0)),
                   pl.BlockSpec((1, 1, d), lambda i: (i, 0, 0))],
        out_shape=[jax.ShapeDtypeStruct((nt, d), F32),
                   jax.ShapeDtypeStruct((nt, d), F32),
                   jax.ShapeDtypeStruct((nt // tm, 1, d), F32)],
        compiler_params=_cparams(("parallel",)),
        name="rope",
    )(qkv, qkv, *tables)


def _moba_prompt_kernel(q_ref, k_ref, vt_ref, km_ref, o_ref):
    qi = pl.program_id(2)
    blk = MOBA_BLOCK
    nblk = km_ref.shape[0]
    q = q_ref[...] * (A_HD ** -0.5)
    lane = lax.broadcasted_iota(I32, (blk, 128), 1)
    key_i = lax.broadcasted_iota(I32, (blk, blk), 0)
    qry_i = lax.broadcasted_iota(I32, (blk, blk), 1)
    jrow = lax.broadcasted_iota(I32, (nblk, blk), 0)
    start = pl.multiple_of(qi * blk, blk)
    k_own = k_ref[pl.ds(start, blk), :].astype(BF16)
    vt_own = vt_ref[qi].astype(BF16)

    qbs, sels, init = [], [], []
    for hh in range(2):
        qh = jnp.where((lane // A_HD) == hh, q, 0.0)
        qb = qh.astype(BF16)
        gate = lax.dot_general(km_ref[...], qh, NT_DIMS, precision=HIGHEST,
                               preferred_element_type=F32)
        valid = jrow < qi
        gm = jnp.where(valid, gate, NEG)
        rank = jnp.zeros((nblk, blk), I32)
        for jp in range(nblk - 1):
            gj = gm[jp:jp + 1, :]
            beats = (gj > gm) | ((gj == gm) & (jp < jrow))
            rank = rank + jnp.where(beats, 1, 0)
        sels.append(jnp.where(valid & (rank < MOBA_TOPK), 1.0, 0.0))
        qbs.append(qb)

        s = lax.dot_general(k_own, qb, NT_DIMS, preferred_element_type=F32)
        s = jnp.where(key_i <= qry_i, s, NEG)
        m0 = jnp.max(s, axis=0, keepdims=True)
        p = jnp.exp(s - m0)
        init += [m0, jnp.sum(p, axis=0, keepdims=True),
                 jnp.dot(vt_own, p.astype(BF16), preferred_element_type=F32)]

    def body(j, carry):
        kj = k_ref[pl.ds(pl.multiple_of(j * blk, blk), blk), :].astype(BF16)
        vtj = vt_ref[j].astype(BF16)
        new = []
        for hh in range(2):
            m, l, acc = carry[3 * hh:3 * hh + 3]
            sj = lax.dot_general(kj, qbs[hh], NT_DIMS, preferred_element_type=F32)
            selj = jnp.sum(jnp.where(jrow == j, sels[hh], 0.0), axis=0, keepdims=True)
            sj = jnp.where(selj > 0.0, sj, NEG)
            m_new = jnp.maximum(m, jnp.max(sj, axis=0, keepdims=True))
            alpha = jnp.exp(m - m_new)
            pj = jnp.exp(sj - m_new)
            new += [m_new, alpha * l + jnp.sum(pj, axis=0, keepdims=True),
                    alpha * acc + jnp.dot(vtj, pj.astype(BF16), preferred_element_type=F32)]
        return tuple(new)

    fin = lax.fori_loop(0, qi, body, tuple(init))
    drow = lax.broadcasted_iota(I32, (128, blk), 0)
    out_t = jnp.where(drow < A_HD, fin[2] / fin[1], fin[5] / fin[4])
    o_ref[...] = out_t.T


def _moba_prompt(q_rot, k_rot, qkv, kmean, nb, seq, d):
    nq = seq // MOBA_BLOCK
    ng = d // 128
    v_t = qkv[:nb * seq, 2 * d:].reshape(nb * nq, MOBA_BLOCK, d).transpose(0, 2, 1)
    return pl.pallas_call(
        _moba_prompt_kernel,
        grid=(nb, ng, nq),
        in_specs=[pl.BlockSpec((MOBA_BLOCK, 128), lambda b, g, i: (b * nq + i, g)),
                  pl.BlockSpec((seq, 128), lambda b, g, i: (b, g)),
                  pl.BlockSpec((nq, 128, MOBA_BLOCK), lambda b, g, i: (b, g, 0)),
                  pl.BlockSpec((nq, 128), lambda b, g, i: (b, g))],
        out_specs=pl.BlockSpec((MOBA_BLOCK, 128), lambda b, g, i: (b * nq + i, g)),
        out_shape=jax.ShapeDtypeStruct((nb * seq, d), F32),
        compiler_params=_cparams(("parallel", "parallel", "arbitrary")),
        name="moba_prompt",
    )(q_rot, k_rot, v_t, kmean)


def _moba_sample_kernel(pt_ref, q_ref, kn_ref, vn_ref, kp_ref, vp_ref, o_ref,
                        m_sc, l_sc, g_sc, acc_sc, *, ppb):
    p = pl.program_id(1)
    rows = q_ref.shape[1]
    n_blk = acc_sc.shape[0]
    kh = PAGE_SIZE * A_HEADS

    @pl.when(p == 0)
    def _():
        m_sc[...] = jnp.full(m_sc.shape, NEG, F32)
        l_sc[...] = jnp.zeros_like(l_sc)
        g_sc[...] = jnp.zeros_like(g_sc)
        acc_sc[...] = jnp.zeros_like(acc_sc)

    qb = (q_ref[0] * (A_HD ** -0.5)).astype(BF16)
    kp = kp_ref[0].reshape(kh, A_HD).astype(BF16)
    vp = vp_ref[0].reshape(kh, A_HD).astype(BF16)
    s = lax.dot_general(qb, kp, NT_DIMS, preferred_element_type=F32)
    r_i = lax.broadcasted_iota(I32, (rows, kh), 0)
    c_i = lax.broadcasted_iota(I32, (rows, kh), 1)
    same_head = (c_i % A_HEADS) == (r_i % A_HEADS)
    j = p // ppb
    col = lax.broadcasted_iota(I32, (rows, n_blk), 1) == j
    m_old = jnp.sum(jnp.where(col, m_sc[...], 0.0), axis=1, keepdims=True)
    l_old = jnp.sum(jnp.where(col, l_sc[...], 0.0), axis=1, keepdims=True)
    sm = jnp.where(same_head, s, NEG)
    m_new = jnp.maximum(m_old, jnp.max(sm, axis=1, keepdims=True))
    alpha = jnp.exp(m_old - m_new)
    pw = jnp.exp(sm - m_new)
    l_new = alpha * l_old + jnp.sum(pw, axis=1, keepdims=True)
    gsum = jnp.sum(jnp.where(same_head, s, 0.0), axis=1, keepdims=True)
    m_sc[...] = jnp.where(col, m_new, m_sc[...])
    l_sc[...] = jnp.where(col, l_new, l_sc[...])
    g_sc[...] = jnp.where(col, g_sc[...] + gsum, g_sc[...])
    acc_sc[j] = alpha * acc_sc[j] + jnp.dot(pw.astype(BF16), vp, preferred_element_type=F32)

    @pl.when(p == pl.num_programs(1) - 1)
    def _():
        gate = g_sc[...]
        jidx = lax.broadcasted_iota(I32, (rows, n_blk), 1)
        sel = jnp.zeros((rows, n_blk), F32)
        for _ in range(min(MOBA_TOPK, n_blk)):
            gmax = jnp.max(gate, axis=1, keepdims=True)
            first = jnp.min(jnp.where(gate == gmax, jidx, n_blk), axis=1, keepdims=True)
            hit = jidx == first
            sel = jnp.where(hit, 1.0, sel)
            gate = jnp.where(hit, -jnp.inf, gate)
        so = lax.dot_general(qb, kn_ref[0].astype(BF16), NT_DIMS, preferred_element_type=F32)
        ro = lax.broadcasted_iota(I32, (rows, rows), 0)
        co = lax.broadcasted_iota(I32, (rows, rows), 1)
        ok = ((co % A_HEADS) == (ro % A_HEADS)) & (co // A_HEADS <= ro // A_HEADS)
        so = jnp.where(ok, so, NEG)
        m_blk = jnp.where(sel > 0.0, m_sc[...], NEG)
        m_tot = jnp.maximum(jnp.max(so, axis=1, keepdims=True), jnp.max(m_blk, axis=1, keepdims=True))
        po = jnp.exp(so - m_tot)
        wgt = jnp.where(sel > 0.0, jnp.exp(m_blk - m_tot), 0.0)
        l_tot = jnp.sum(po, axis=1, keepdims=True) + jnp.sum(wgt * l_sc[...], axis=1, keepdims=True)
        acc = jnp.dot(po.astype(BF16), vn_ref[0].astype(BF16), preferred_element_type=F32)
        for jb in range(n_blk):
            acc = acc + wgt[:, jb:jb + 1] * acc_sc[jb]
        o_ref[0] = acc / l_tot


def _moba_sample(q_s, k_s, v_s, cache_k, cache_v, page_table_flat, ppb):
    nb, rows, hd = q_s.shape
    n_pages = page_table_flat.shape[0] // nb
    n_blk = n_pages // ppb
    page_map = lambda b, p, pt: (pt[b * n_pages + p], 0, 0, 0)
    seq_map = lambda b, p, pt: (b, 0, 0)
    grid_spec = pltpu.PrefetchScalarGridSpec(
        num_scalar_prefetch=1,
        grid=(nb, n_pages),
        in_specs=[pl.BlockSpec((1, rows, hd), seq_map),
                  pl.BlockSpec((1, rows, hd), seq_map),
                  pl.BlockSpec((1, rows, hd), seq_map),
                  pl.BlockSpec((1, PAGE_SIZE, A_HEADS, hd), page_map),
                  pl.BlockSpec((1, PAGE_SIZE, A_HEADS, hd), page_map)],
        out_specs=pl.BlockSpec((1, rows, hd), seq_map),
        scratch_shapes=[pltpu.VMEM((rows, n_blk), F32),
                        pltpu.VMEM((rows, n_blk), F32),
                        pltpu.VMEM((rows, n_blk), F32),
                        pltpu.VMEM((n_blk, rows, hd), F32)],
    )
    return pl.pallas_call(
        functools.partial(_moba_sample_kernel, ppb=ppb),
        grid_spec=grid_spec,
        out_shape=jax.ShapeDtypeStruct((nb, rows, hd), F32),
        compiler_params=_cparams(("parallel", "arbitrary")),
        name="moba_sample",
    )(page_table_flat, q_s, k_s, v_s, cache_k, cache_v)


def _top16_desc(vals, payload, n_rows):
    t = vals.shape[1]
    r16 = lax.broadcasted_iota(I32, (P_TOPK, t), 0)
    out_v = jnp.zeros((P_TOPK, t), F32)
    out_p = jnp.zeros((P_TOPK, t), I32)
    big = jnp.int32(2 ** 30)
    for r in range(P_TOPK):
        m = jnp.max(vals, axis=0, keepdims=True)
        pick = jnp.min(jnp.where(vals == m, payload, big), axis=0, keepdims=True)
        vals = jnp.where(payload == pick, -jnp.inf, vals)
        out_v = jnp.where(r16 == r, m, out_v)
        out_p = jnp.where(r16 == r, pick, out_p)
    return out_v, out_p


def _peer_topk_kernel(q_ref, keys_ref, pidx_ref, par_ref, g_ref):
    t = q_ref.shape[0]
    kio = lax.broadcasted_iota(I32, (P_NKEYS, t), 0)
    e_heads, g_heads = [], []
    for h in range(P_HEADS):
        sv, si = [], []
        for p in range(2):
            c0 = (h * 2 + p) * (P_NKEYS)
            scores = lax.dot_general(keys_ref[h, p], q_ref[:, c0:c0 + 128], NT_DIMS,
                                     precision=HIGHEST, preferred_element_type=F32)
            v, i = _top16_desc(scores, kio, P_NKEYS)
            sv.append(v)
            si.append(i)
        half = P_TOPK // 2
        cand = jnp.concatenate(
            [sv[0][0:1, :] + sv[1]]
            + [sv[0][i:i + 1, :] + sv[1][0:half, :] for i in range(1, half)]
            + [sv[0][half:, :] + sv[1][0:1, :]], axis=0)
        eid = jnp.concatenate(
            [si[0][0:1, :] * P_NKEYS + si[1]]
            + [si[0][i:i + 1, :] * P_NKEYS + si[1][0:half, :] for i in range(1, half)]
            + [si[0][half:, :] * P_NKEYS + si[1][0:1, :]], axis=0)
        cv, ce = _top16_desc(cand, eid, cand.shape[0])
        ex = jnp.exp(cv - cv[0:1, :])
        g_heads.append(ex / jnp.sum(ex, axis=0, keepdims=True))
        e_heads.append(ce)
    e_all = jnp.concatenate(e_heads, axis=0).T
    pidx_ref[...] = e_all >> 1
    par_ref[...] = (e_all & 1).astype(F32)
    g_ref[...] = jnp.concatenate(g_heads, axis=0).T


def _peer_topk(q, keys):
    nt = q.shape[0]
    t = TOK_BLOCK
    return pl.pallas_call(
        _peer_topk_kernel,
        grid=(nt // t,),
        in_specs=[pl.BlockSpec((t, q.shape[1]), lambda i: (i, 0)),
                  pl.BlockSpec(keys.shape, lambda i: (0, 0, 0, 0))],
        out_specs=[pl.BlockSpec((t, P_SLOTS), lambda i: (i, 0)),
                   pl.BlockSpec((t, P_SLOTS), lambda i: (i, 0)),
                   pl.BlockSpec((t, P_SLOTS), lambda i: (i, 0))],
        out_shape=[jax.ShapeDtypeStruct((nt, P_SLOTS), I32),
                   jax.ShapeDtypeStruct((nt, P_SLOTS), F32),
                   jax.ShapeDtypeStruct((nt, P_SLOTS), F32)],
        compiler_params=_cparams(("parallel",)),
        name="peer_topk",
    )(q, keys)


GATHER_CHUNK = 16
TILE_ROWS = 16
SLOT_COLS = P_SLOTS * TILE_ROWS


def _gather_tiles(tab_ref, pidx_ref, n, c):
    return jnp.concatenate(
        [tab_ref[pidx_ref[n, c * GATHER_CHUNK + i]] for i in range(GATHER_CHUNK)], axis=0)


def _split_bf16(x):
    hi = x.astype(BF16)
    lo = (x - hi.astype(F32)).astype(BF16)
    return jnp.concatenate([hi, lo], axis=0)


def _fold_row_mask(par_row, width):
    r = (lax.broadcasted_iota(I32, (8, width), 1) % TILE_ROWS).astype(F32)
    c = lax.broadcasted_iota(I32, (8, width), 0).astype(F32)
    return (r - 8.0 * par_row) == c


def _peer_u_kernel(pidx_ref, xf_ref, g_ref, par_ref, tab_ref, exp_ref, w_ref, parx_sc, hid_sc):
    tb = g_ref.shape[0]
    n_chunk = P_SLOTS // GATHER_CHUNK
    cw = GATHER_CHUNK * TILE_ROWS
    parx_sc[...] = jnp.dot(par_ref[...].astype(BF16), exp_ref[...], preferred_element_type=F32)

    def group(g8, carry):
        base = pl.multiple_of(g8 * 8, 8)
        par_blk = parx_sc[pl.ds(base, 8), :]
        rows = []
        for j in range(8):
            n = base + j
            x16 = _split_bf16(xf_ref[pl.ds(pl.multiple_of(n * 8, 8), 8), :])
            parts = []
            for c in range(n_chunk):
                y = lax.dot_general(x16, _gather_tiles(tab_ref, pidx_ref, n, c), NT_DIMS,
                                    preferred_element_type=F32)
                keep = _fold_row_mask(par_blk[j:j + 1, c * cw:(c + 1) * cw], cw)
                parts.append(jnp.sum(jnp.where(keep, y[0:8] + y[8:16], 0.0), axis=0, keepdims=True))
            rows.append(jnp.concatenate(parts, axis=1))
        s16 = _split_bf16(jnp.concatenate(rows, axis=0))
        h = lax.dot_general(s16, exp_ref[...], NT_DIMS, preferred_element_type=F32)
        hid_sc[pl.ds(base, 8), :] = h[0:8] + h[8:16]
        return carry

    lax.fori_loop(0, tb // 8, group, 0)
    hid = hid_sc[...]
    gelu = 0.5 * hid * (1.0 + lax.erf(hid * (2.0 ** -0.5)))
    w_ref[...] = g_ref[...] * gelu


def _peer_v_kernel(pidx_ref, w_ref, par_ref, tab_ref, exp_ref, o_ref, wx_sc, parx_sc):
    tb = w_ref.shape[0]
    n_chunk = P_SLOTS // GATHER_CHUNK
    cw = GATHER_CHUNK * TILE_ROWS
    parx_sc[...] = jnp.dot(par_ref[...].astype(BF16), exp_ref[...], preferred_element_type=F32)
    w = w_ref[...]
    w_hi = w.astype(BF16)
    w_lo = (w - w_hi.astype(F32)).astype(BF16)
    wx_sc[...] = (jnp.dot(w_hi, exp_ref[...], preferred_element_type=F32)
                  + jnp.dot(w_lo, exp_ref[...], preferred_element_type=F32))

    def group(g8, carry):
        base = pl.multiple_of(g8 * 8, 8)
        par_blk = parx_sc[pl.ds(base, 8), :]
        w_blk = wx_sc[pl.ds(base, 8), :]
        for j in range(8):
            n = base + j
            keep = _fold_row_mask(par_blk[j:j + 1, :], SLOT_COLS)
            lhs = _split_bf16(jnp.where(keep, w_blk[j:j + 1, :], 0.0))
            acc = jnp.zeros((16, 128), F32)
            for c in range(n_chunk):
                acc = acc + jnp.dot(lhs[:, c * cw:(c + 1) * cw], _gather_tiles(tab_ref, pidx_ref, n, c),
                                    preferred_element_type=F32)
            o_ref[pl.ds(pl.multiple_of(n * 8, 8), 8), :] = acc[0:8] + acc[8:16]
        return carry

    lax.fori_loop(0, tb // 8, group, 0)


def _peer_gather(x, pidx, par, g, u_pairs, v_pairs):
    nt, d = x.shape
    fold = d // 128
    tb = TOK_BLOCK
    n_pair = u_pairs.shape[0]
    tab_spec = pl.BlockSpec((n_pair, TILE_ROWS, 128), lambda i: (0, 0, 0), pipeline_mode=pl.Buffered(1))
    smem_spec = pl.BlockSpec((tb, P_SLOTS), lambda i: (i, 0), memory_space=pltpu.SMEM)
    slot_spec = pl.BlockSpec((tb, P_SLOTS), lambda i: (i, 0))
    exp_spec = pl.BlockSpec((P_SLOTS, SLOT_COLS), lambda i: (0, 0))
    expand = (jnp.arange(SLOT_COLS)[None, :] // TILE_ROWS == jnp.arange(P_SLOTS)[:, None]).astype(BF16)
    w = pl.pallas_call(
        _peer_u_kernel,
        grid=(nt // tb,),
        in_specs=[smem_spec, pl.BlockSpec((tb * fold, 128), lambda i: (i, 0)), slot_spec, slot_spec,
                  tab_spec, exp_spec],
        out_specs=slot_spec,
        out_shape=jax.ShapeDtypeStruct((nt, P_SLOTS), F32),
        scratch_shapes=[pltpu.VMEM((tb, SLOT_COLS), F32),
                        pltpu.VMEM((tb, P_SLOTS), F32)],
        compiler_params=_cparams(("parallel",)),
        name="peer_u",
    )(pidx, x.reshape(nt * fold, 128), g, par, u_pairs, expand)
    out = pl.pallas_call(
        _peer_v_kernel,
        grid=(nt // tb,),
        in_specs=[smem_spec, slot_spec, slot_spec, tab_spec, exp_spec],
        out_specs=pl.BlockSpec((tb * fold, 128), lambda i: (i, 0)),
        out_shape=jax.ShapeDtypeStruct((nt * fold, 128), F32),
        scratch_shapes=[pltpu.VMEM((tb, SLOT_COLS), F32),
                        pltpu.VMEM((tb, SLOT_COLS), F32)],
        compiler_params=_cparams(("parallel",)),
        name="peer_v",
    )(pidx, w, par, v_pairs, expand)
    return out.reshape(nt, d)


def _pair_table(tab):
    e, d = tab.shape
    return tab.astype(BF16).reshape(e // 2, 2 * (d // 128), 128)


def _peer_layer(x, w_q, keys, u_tab, v_tab, ln_g, ln_b):
    q = _matmul(x, w_q.astype(BF16))
    pidx, par, g = _peer_topk(q, keys)
    y = _peer_gather(x, pidx, par, g, _pair_table(u_tab), _pair_table(v_tab))
    return _res_ln(x, y, ln_g, ln_b)


def kernel(x_prompt, x_sample, state_mlstm_C, state_mlstm_n, state_mlstm_m, cache_k, cache_v, page_table,
           w_in_a, b_gate_a, mh_norm_a, w_out_a, w_kv, w_q_b, w_out_b,
           ln_mix_g, ln_mix_b, ln_ffn_g, ln_ffn_b, peer_wq, peer_keys, peer_u, peer_v):
    nb_p, seq_p, d = x_prompt.shape
    nb_s, seq_s, _ = x_sample.shape
    n_p, n_s = nb_p * seq_p, nb_s * seq_s
    nt = -(-(n_p + n_s) // ROW_TILE) * ROW_TILE
    past_len = page_table.shape[1] * PAGE_SIZE
    ppb = MOBA_BLOCK // PAGE_SIZE
    assert past_len % MOBA_BLOCK == 0 and seq_p % MOBA_BLOCK == 0 and n_p % ROW_TILE == 0
    assert state_mlstm_C.shape[0] == 1 and w_q_b.shape[0] == 1

    x = jnp.concatenate([x_prompt.reshape(n_p, d), x_sample.reshape(n_s, d),
                         jnp.zeros((nt - n_p - n_s, d), F32)], axis=0)

    hn, states = _mlstm_layer(x, n_p, nb_p, seq_p, nb_s, seq_s,
                              state_mlstm_C[0], state_mlstm_n[0], state_mlstm_m[0],
                              w_in_a[0], b_gate_a[0], mh_norm_a[0])
    x = _matmul_res_ln(hn, w_out_a[0].astype(BF16), x, ln_mix_g[0], ln_mix_b[0])
    x = _peer_layer(x, peer_wq[0], peer_keys[0], peer_u[0], peer_v[0], ln_ffn_g[0], ln_ffn_b[0])

    qkv = _matmul(x, jnp.concatenate([w_q_b[0], w_kv], axis=1).astype(BF16))
    pos = jnp.concatenate([jnp.tile(jnp.arange(seq_p), nb_p),
                           past_len + jnp.tile(jnp.arange(seq_s), nb_s),
                           jnp.zeros((nt - n_p - n_s,), I32)])
    q_rot, k_rot, kmean = _rope(qkv, _rope_tables(pos), d)
    attn_p = _moba_prompt(q_rot, k_rot, qkv, kmean.reshape(nt // MOBA_BLOCK, d), nb_p, seq_p, d)

    q_s = q_rot[n_p:n_p + n_s].reshape(nb_s, seq_s, d)
    k_s = k_rot[n_p:n_p + n_s].reshape(nb_s, seq_s, d)
    v_s = qkv[n_p:n_p + n_s, 2 * d:].reshape(nb_s, seq_s, d)
    pt_flat = page_table.reshape(-1).astype(I32)
    rows_s = seq_s * A_HEADS
    attn_s = _moba_sample(q_s.reshape(nb_s, rows_s, A_HD), k_s.reshape(nb_s, rows_s, A_HD),
                          v_s.reshape(nb_s, rows_s, A_HD), cache_k, cache_v, pt_flat, ppb)

    attn = jnp.concatenate([attn_p, attn_s.reshape(n_s, d), jnp.zeros((nt - n_p - n_s, d), F32)], axis=0)
    x = _matmul_res_ln(attn, w_out_b[0].astype(BF16), x, ln_mix_g[1], ln_mix_b[1])
    x = _peer_layer(x, peer_wq[1], peer_keys[1], peer_u[1], peer_v[1], ln_ffn_g[1], ln_ffn_b[1])

    y_prompt = x[:n_p].reshape(nb_p, seq_p, d)
    y_sample = x[n_p:n_p + n_s].reshape(nb_s, seq_s, d)
    c_p, n_pr, m_p, c_s, n_sm, m_s = states
    k_prompt = k_rot[:n_p].reshape(nb_p, seq_p, A_HEADS, A_HD)
    v_prompt = qkv[:n_p, 2 * d:].reshape(nb_p, seq_p, A_HEADS, A_HD)
    return (y_prompt, y_sample, c_p[None], n_pr[None], m_p[None], c_s[None], n_sm[None], m_s[None],
            k_prompt, v_prompt, k_s.reshape(nb_s, seq_s, A_HEADS, A_HD), v_s.reshape(nb_s, seq_s, A_HEADS, A_HD))
```

```python
import functools

import jax
import jax.numpy as jnp
from jax import lax
from jax.experimental import pallas as pl
from jax.experimental.pallas import tpu as pltpu

F32 = jnp.float32
BF16 = jnp.bfloat16
I32 = jnp.int32
HIGHEST = lax.Precision.HIGHEST

DEPTH = 2
DN_ALPHA = (2.0 * DEPTH) ** 0.25
LN_EPS = 1e-5

M_HEADS = 8
M_DK = 64
M_DV = 128
MLSTM_CHUNK = 256
MLSTM_SAMPLE_CHUNK = 128

A_HEADS = 16
A_HD = 64
ROT_DIM = 16
ROPE_THETA = 500000.0
MOBA_BLOCK = 256
MOBA_TOPK = 3
PAGE_SIZE = 128

P_HEADS = 8
P_NKEYS = 128
P_TOPK = 16
P_SLOTS = P_HEADS * P_TOPK

ROW_TILE = 512
TOK_BLOCK = 128
NEG = -1e30
VMEM_LIMIT = 56 * 1024 * 1024

NT_DIMS = (((1,), (1,)), ((), ()))


def _cparams(semantics, vmem=VMEM_LIMIT):
    return pltpu.CompilerParams(dimension_semantics=semantics, vmem_limit_bytes=vmem)


def _mm_kernel(x_ref, w_ref, o_ref):
    o_ref[...] = jnp.dot(x_ref[...].astype(BF16), w_ref[...], preferred_element_type=F32)


def _matmul(x, w_bf16, tm=ROW_TILE):
    m, k = x.shape
    n = w_bf16.shape[1]
    return pl.pallas_call(
        _mm_kernel,
        grid=(m // tm,),
        in_specs=[pl.BlockSpec((tm, k), lambda i: (i, 0)),
                  pl.BlockSpec((k, n), lambda i: (0, 0))],
        out_specs=pl.BlockSpec((tm, n), lambda i: (i, 0)),
        out_shape=jax.ShapeDtypeStruct((m, n), F32),
        compiler_params=_cparams(("parallel",)),
        name="matmul",
    )(x, w_bf16)


def _layer_norm_rows(z, g, b):
    mu = jnp.mean(z, axis=-1, keepdims=True)
    zc = z - mu
    var = jnp.mean(zc * zc, axis=-1, keepdims=True)
    return zc * lax.rsqrt(var + LN_EPS) * g + b


def _mm_ln_kernel(a_ref, w_ref, x_ref, g_ref, b_ref, o_ref):
    y = jnp.dot(a_ref[...].astype(BF16), w_ref[...], preferred_element_type=F32)
    o_ref[...] = _layer_norm_rows(DN_ALPHA * x_ref[...] + y, g_ref[...], b_ref[...])


def _matmul_res_ln(a, w_bf16, x_res, g, b, tm=ROW_TILE):
    m, k = a.shape
    n = w_bf16.shape[1]
    return pl.pallas_call(
        _mm_ln_kernel,
        grid=(m // tm,),
        in_specs=[pl.BlockSpec((tm, k), lambda i: (i, 0)),
                  pl.BlockSpec((k, n), lambda i: (0, 0)),
                  pl.BlockSpec((tm, n), lambda i: (i, 0)),
                  pl.BlockSpec((1, n), lambda i: (0, 0)),
                  pl.BlockSpec((1, n), lambda i: (0, 0))],
        out_specs=pl.BlockSpec((tm, n), lambda i: (i, 0)),
        out_shape=jax.ShapeDtypeStruct((m, n), F32),
        compiler_params=_cparams(("parallel",)),
        name="matmul_res_ln",
    )(a, w_bf16, x_res, g.reshape(1, n), b.reshape(1, n))


def _res_ln_kernel(x_ref, y_ref, g_ref, b_ref, o_ref):
    o_ref[...] = _layer_norm_rows(DN_ALPHA * x_ref[...] + y_ref[...], g_ref[...], b_ref[...])


def _res_ln(x, y, g, b, tm=ROW_TILE):
    m, n = x.shape
    return pl.pallas_call(
        _res_ln_kernel,
        grid=(m // tm,),
        in_specs=[pl.BlockSpec((tm, n), lambda i: (i, 0)),
                  pl.BlockSpec((tm, n), lambda i: (i, 0)),
                  pl.BlockSpec((1, n), lambda i: (0, 0)),
                  pl.BlockSpec((1, n), lambda i: (0, 0))],
        out_specs=pl.BlockSpec((tm, n), lambda i: (i, 0)),
        out_shape=jax.ShapeDtypeStruct((m, n), F32),
        compiler_params=_cparams(("parallel",)),
        name="res_ln",
    )(x, y, g.reshape(1, n), b.reshape(1, n))


def _gates_kernel(x_ref, w_ref, b_ref, o_ref):
    g = jnp.dot(x_ref[...], w_ref[...], precision=HIGHEST, preferred_element_type=F32) + b_ref[...]
    lane = lax.broadcasted_iota(I32, g.shape, 1)
    log_sig = jnp.minimum(g, 0.0) - jnp.log1p(jnp.exp(-jnp.abs(g)))
    o_ref[...] = jnp.where(lane < M_HEADS, g, log_sig)


def _mlstm_gates(x, w_gate, b_gate, tm=ROW_TILE):
    m, k = x.shape
    w = jnp.zeros((k, 128), F32).at[:, :2 * M_HEADS].set(w_gate)
    b = jnp.zeros((1, 128), F32).at[0, :2 * M_HEADS].set(b_gate)
    return pl.pallas_call(
        _gates_kernel,
        grid=(m // tm,),
        in_specs=[pl.BlockSpec((tm, k), lambda i: (i, 0)),
                  pl.BlockSpec((k, 128), lambda i: (0, 0)),
                  pl.BlockSpec((1, 128), lambda i: (0, 0))],
        out_specs=pl.BlockSpec((tm, 128), lambda i: (i, 0)),
        out_shape=jax.ShapeDtypeStruct((m, 128), F32),
        compiler_params=_cparams(("parallel",)),
        name="mlstm_gates",
    )(x, w, b)


def _mlstm_kernel(q_ref, kt_ref, v_ref, o_ref, gc_ref, gr_ref, c0_ref, m0_ref, gain_ref,
                  hn_ref, cout_ref, mout_ref, c_sc, m_sc):
    c = pl.program_id(1)
    L = q_ref.shape[0]

    @pl.when(c == 0)
    def _():
        c_sc[...] = c0_ref[0]
        m_sc[...] = m0_ref[0]

    gc = gc_ref[...]
    gr = gr_ref[0]
    row = lax.broadcasted_iota(I32, (L, L), 0)
    col = lax.broadcasted_iota(I32, (L, L), 1)
    causal = col <= row
    b_col = jnp.dot(causal.astype(F32), gc, precision=HIGHEST, preferred_element_type=F32)
    b_row = jnp.dot(gr, (row <= col).astype(F32), precision=HIGHEST, preferred_element_type=F32)
    ones_lane0 = (lax.broadcasted_iota(I32, (L, 128), 1) == 0).astype(BF16)

    for h in range(M_HEADS):
        bc = b_col[:, M_HEADS + h:M_HEADS + h + 1]
        br = b_row[M_HEADS + h:M_HEADS + h + 1, :]
        igr = gr[h:h + 1, :]
        m_prev = m_sc[h:h + 1, 0:1]
        dlog = jnp.where(causal, bc - br + igr, NEG)
        m_inter = bc + m_prev
        m_t = jnp.maximum(m_inter, jnp.max(dlog, axis=1, keepdims=True))
        w_intra = jnp.exp(dlog - m_t)
        w_inter = jnp.exp(m_inter - m_t)

        qh = (q_ref[:, h * M_DK:(h + 1) * M_DK] * (M_DK ** -0.5)).astype(BF16)
        kth = kt_ref[0, h * M_DK:(h + 1) * M_DK, :]
        v_ext = jnp.concatenate(
            [v_ref[:, h * M_DV:(h + 1) * M_DV].astype(BF16), ones_lane0], axis=1)
        s = jnp.dot(qh, kth.astype(BF16), preferred_element_type=F32) * w_intra
        c_h = c_sc[h]
        nd = (jnp.dot(s.astype(BF16), v_ext, preferred_element_type=F32)
              + w_inter * jnp.dot(qh, c_h.astype(BF16), preferred_element_type=F32))
        num = nd[:, :M_DV]
        den = nd[:, M_DV:M_DV + 1]
        hh = num / jnp.maximum(jnp.abs(den), jnp.exp(-m_t))
        hh = hh * lax.rsqrt(jnp.mean(hh * hh, axis=1, keepdims=True) + 1e-6)
        og = o_ref[:, h * M_DV:(h + 1) * M_DV]
        hn_ref[:, h * M_DV:(h + 1) * M_DV] = (
            hh * gain_ref[:, h * M_DV:(h + 1) * M_DV] * (1.0 / (1.0 + jnp.exp(-og))))

        m_new = m_t[L - 1:L, :]
        b_last = bc[L - 1:L, :]
        decay = jnp.exp(b_last + m_prev - m_new)
        w_s = jnp.exp(b_last - br + igr - m_new)
        kw = (kth * w_s).astype(BF16)
        c_sc[h] = decay * c_h + jnp.dot(kw, v_ext, preferred_element_type=F32)
        m_sc[h:h + 1, :] = jnp.broadcast_to(m_new, (1, 128))

    @pl.when(c == pl.num_programs(1) - 1)
    def _():
        cout_ref[0] = c_sc[...]
        mout_ref[0] = m_sc[...]


def _mlstm(proj, kt, gates, gates_t, c0_ext, m0_b, gain, nb, seq, chunk):
    nc = seq // chunk
    qk = M_HEADS * M_DK
    vd = M_HEADS * M_DV
    row_blk = lambda b, c: (b * nc + c, 0)
    return pl.pallas_call(
        _mlstm_kernel,
        grid=(nb, nc),
        in_specs=[pl.BlockSpec((chunk, qk), row_blk),
                  pl.BlockSpec((1, qk, chunk), lambda b, c: (b, 0, c)),
                  pl.BlockSpec((chunk, vd), lambda b, c: (b * nc + c, 1)),
                  pl.BlockSpec((chunk, vd), lambda b, c: (b * nc + c, 2)),
                  pl.BlockSpec((chunk, 128), row_blk),
                  pl.BlockSpec((1, 2 * M_HEADS, chunk), lambda b, c: (b, 0, c)),
                  pl.BlockSpec((1, M_HEADS, M_DK, 2 * M_DV), lambda b, c: (b, 0, 0, 0)),
                  pl.BlockSpec((1, M_HEADS, 128), lambda b, c: (b, 0, 0)),
                  pl.BlockSpec((1, vd), lambda b, c: (0, 0))],
        out_specs=[pl.BlockSpec((chunk, vd), row_blk),
                   pl.BlockSpec((1, M_HEADS, M_DK, 2 * M_DV), lambda b, c: (b, 0, 0, 0)),
                   pl.BlockSpec((1, M_HEADS, 128), lambda b, c: (b, 0, 0))],
        out_shape=[jax.ShapeDtypeStruct((nb * seq, vd), F32),
                   jax.ShapeDtypeStruct((nb, M_HEADS, M_DK, 2 * M_DV), F32),
                   jax.ShapeDtypeStruct((nb, M_HEADS, 128), F32)],
        scratch_shapes=[pltpu.VMEM((M_HEADS, M_DK, 2 * M_DV), F32),
                        pltpu.VMEM((M_HEADS, 128), F32)],
        compiler_params=_cparams(("parallel", "arbitrary")),
        name="mlstm",
    )(proj, kt, proj, proj, gates, gates_t, c0_ext, m0_b, gain.reshape(1, vd))


def _mlstm_layer(x_all, n_p, nb_p, seq_p, nb_s, seq_s, st_c, st_n, st_m, w_in, b_gate, gain):
    nt = x_all.shape[0]
    qk = M_HEADS * M_DK
    vd = M_HEADS * M_DV
    n_s = nb_s * seq_s
    proj = _matmul(x_all, w_in[:, :2 * qk + 2 * vd].astype(BF16))
    gates = _mlstm_gates(x_all, w_in[:, 2 * qk + 2 * vd:], b_gate)

    kt_p = proj[:n_p, qk:2 * qk].reshape(nb_p, seq_p, qk).transpose(0, 2, 1)
    gt_p = gates[:n_p, :2 * M_HEADS].reshape(nb_p, seq_p, 2 * M_HEADS).transpose(0, 2, 1)
    c0_p = jnp.zeros((nb_p, M_HEADS, M_DK, 2 * M_DV), F32)
    m0_p = jnp.zeros((nb_p, M_HEADS, 128), F32)
    hn_p, c_p, m_p = _mlstm(proj, kt_p, gates, gt_p, c0_p, m0_p, gain, nb_p, seq_p,
                            min(MLSTM_CHUNK, seq_p))

    ls = MLSTM_SAMPLE_CHUNK
    pad = ((0, 0), (0, ls - seq_s), (0, 0))
    proj_s = jnp.pad(proj[n_p:n_p + n_s].reshape(nb_s, seq_s, -1), pad)
    g_s = gates[n_p:n_p + n_s].reshape(nb_s, seq_s, 128)
    pad_row = jnp.where(jnp.arange(128) < M_HEADS, NEG, 0.0).astype(F32)
    g_s = jnp.concatenate([g_s, jnp.broadcast_to(pad_row, (nb_s, ls - seq_s, 128))], axis=1)
    kt_s = proj_s[:, :, qk:2 * qk].transpose(0, 2, 1)
    gt_s = g_s[:, :, :2 * M_HEADS].transpose(0, 2, 1)
    c0_s = jnp.concatenate([st_c, st_n[..., None],
                            jnp.zeros(st_c.shape[:3] + (M_DV - 1,), F32)], axis=-1)
    m0_s = jnp.broadcast_to(st_m[..., None], st_m.shape + (128,))
    hn_s, c_s, m_s = _mlstm(proj_s.reshape(nb_s * ls, -1), kt_s, g_s.reshape(nb_s * ls, 128), gt_s,
                            c0_s, m0_s, gain, nb_s, ls, ls)
    hn_s = hn_s.reshape(nb_s, ls, vd)[:, :seq_s].reshape(n_s, vd)
    hn = jnp.concatenate([hn_p, hn_s, jnp.zeros((nt - n_p - n_s, vd), F32)], axis=0)
    states = (c_p[..., :M_DV], c_p[..., M_DV], m_p[..., 0],
              c_s[..., :M_DV], c_s[..., M_DV], m_s[..., 0])
    return hn, states


def _rope_kernel(q_ref, k_ref, ca_ref, cb_ref, cc_ref, qo_ref, ko_ref, km_ref):
    ca, cb, cc = ca_ref[...], cb_ref[...], cc_ref[...]
    half = ROT_DIM // 2
    for j in range(q_ref.shape[1] // 128):
        sl = slice(j * 128, (j + 1) * 128)
        for src, dst in ((q_ref, qo_ref), (k_ref, ko_ref)):
            x = src[:, sl]
            dst[:, sl] = (x * ca + pltpu.roll(x, 128 - half, 1) * cb + pltpu.roll(x, half, 1) * cc)
    km_ref[0] = jnp.mean(ko_ref[...], axis=0, keepdims=True)


def _rope_tables(pos):
    half = ROT_DIM // 2
    inv = ROPE_THETA ** (-jnp.arange(0, ROT_DIM, 2, dtype=F32) / ROT_DIM)
    ang = pos.astype(F32)[:, None] * inv[None, :]
    cos, sin = jnp.cos(ang), jnp.sin(ang)
    lane = jnp.arange(128) % A_HD
    f = lane % half
    in_lo = lane < half
    in_hi = (lane >= half) & (lane < ROT_DIM)
    ca = jnp.where((in_lo | in_hi)[None, :], cos[:, f], 1.0)
    cb = jnp.where(in_lo[None, :], -sin[:, f], 0.0)
    cc = jnp.where(in_hi[None, :], sin[:, f], 0.0)
    return ca.astype(F32), cb.astype(F32), cc.astype(F32)


def _rope(qkv, tables, d):
    nt = qkv.shape[0]
    tm = MOBA_BLOCK
    spec_t = pl.BlockSpec((tm, 128), lambda i: (i, 0))
    return pl.pallas_call(
        _rope_kernel,
        grid=(nt // tm,),
        in_specs=[pl.BlockSpec((tm, d), lambda i: (i, 0)),
                  pl.BlockSpec((tm, d), lambda i: (i, 1)),
                  spec_t, spec_t, spec_t],
        out_specs=[pl.BlockSpec((tm, d), lambda i: (i, 0)),
                   pl.BlockSpec((tm, d), lambda i: (i, 0)),
                   pl.BlockSpec((1, 1, d), lambda i: (i, 0, 0))],
        out_shape=[jax.ShapeDtypeStruct((nt, d), F32),
                   jax.ShapeDtypeStruct((nt, d), F32),
                   jax.ShapeDtypeStruct((nt // tm, 1, d), F32)],
        compiler_params=_cparams(("parallel",)),
        name="rope",
    )(qkv, qkv, *tables)


def _moba_prompt_kernel(q_ref, k_ref, vt_ref, km_ref, o_ref):
    qi = pl.program_id(2)
    blk = MOBA_BLOCK
    nblk = km_ref.shape[0]
    q = q_ref[...] * (A_HD ** -0.5)
    lane = lax.broadcasted_iota(I32, (blk, 128), 1)
    key_i = lax.broadcasted_iota(I32, (blk, blk), 0)
    qry_i = lax.broadcasted_iota(I32, (blk, blk), 1)
    jrow = lax.broadcasted_iota(I32, (nblk, blk), 0)
    start = pl.multiple_of(qi * blk, blk)
    k_own = k_ref[pl.ds(start, blk), :].astype(BF16)
    vt_own = vt_ref[qi].astype(BF16)

    qbs, sels, init = [], [], []
    for hh in range(2):
        qh = jnp.where((lane // A_HD) == hh, q, 0.0)
        qb = qh.astype(BF16)
        gate = lax.dot_general(km_ref[...], qh, NT_DIMS, precision=HIGHEST,
                               preferred_element_type=F32)
        valid = jrow < qi
        gm = jnp.where(valid, gate, NEG)
        rank = jnp.zeros((nblk, blk), I32)
        for jp in range(nblk - 1):
            gj = gm[jp:jp + 1, :]
            beats = (gj > gm) | ((gj == gm) & (jp < jrow))
            rank = rank + jnp.where(beats, 1, 0)
        sels.append(jnp.where(valid & (rank < MOBA_TOPK), 1.0, 0.0))
        qbs.append(qb)

        s = lax.dot_general(k_own, qb, NT_DIMS, preferred_element_type=F32)
        s = jnp.where(key_i <= qry_i, s, NEG)
        m0 = jnp.max(s, axis=0, keepdims=True)
        p = jnp.exp(s - m0)
        init += [m0, jnp.sum(p, axis=0, keepdims=True),
                 jnp.dot(vt_own, p.astype(BF16), preferred_element_type=F32)]

    def body(j, carry):
        kj = k_ref[pl.ds(pl.multiple_of(j * blk, blk), blk), :].astype(BF16)
        vtj = vt_ref[j].astype(BF16)
        new = []
        for hh in range(2):
            m, l, acc = carry[3 * hh:3 * hh + 3]
            sj = lax.dot_general(kj, qbs[hh], NT_DIMS, preferred_element_type=F32)
            selj = jnp.sum(jnp.where(jrow == j, sels[hh], 0.0), axis=0, keepdims=True)
            sj = jnp.where(selj > 0.0, sj, NEG)
            m_new = jnp.maximum(m, jnp.max(sj, axis=0, keepdims=True))
            alpha = jnp.exp(m - m_new)
            pj = jnp.exp(sj - m_new)
            new += [m_new, alpha * l + jnp.sum(pj, axis=0, keepdims=True),
                    alpha * acc + jnp.dot(vtj, pj.astype(BF16), preferred_element_type=F32)]
        return tuple(new)

    fin = lax.fori_loop(0, qi, body, tuple(init))
    drow = lax.broadcasted_iota(I32, (128, blk), 0)
    out_t = jnp.where(drow < A_HD, fin[2] / fin[1], fin[5] / fin[4])
    o_ref[...] = out_t.T


def _moba_prompt(q_rot, k_rot, qkv, kmean, nb, seq, d):
    nq = seq // MOBA_BLOCK
    ng = d // 128
    v_t = qkv[:nb * seq, 2 * d:].reshape(nb * nq, MOBA_BLOCK, d).transpose(0, 2, 1)
    return pl.pallas_call(
        _moba_prompt_kernel,
        grid=(nb, ng, nq),
        in_specs=[pl.BlockSpec((MOBA_BLOCK, 128), lambda b, g, i: (b * nq + i, g)),
                  pl.BlockSpec((seq, 128), lambda b, g, i: (b, g)),
                  pl.BlockSpec((nq, 128, MOBA_BLOCK), lambda b, g, i: (b, g, 0)),
                  pl.BlockSpec((nq, 128), lambda b, g, i: (b, g))],
        out_specs=pl.BlockSpec((MOBA_BLOCK, 128), lambda b, g, i: (b * nq + i, g)),
        out_shape=jax.ShapeDtypeStruct((nb * seq, d), F32),
        compiler_params=_cparams(("parallel", "parallel", "arbitrary")),
        name="moba_prompt",
    )(q_rot, k_rot, v_t, kmean)


HEAD_BATCH = (((2,), (1,)), ((0,), (0,)))
HEAD_BATCH_NT = (((2,), (2,)), ((0,), (0,)))
T_PAD = 8


def _moba_sample_kernel(pt_ref, q_ref, kn_ref, vn_ref, k0_ref, k1_ref, v0_ref, v1_ref, o_ref,
                        m_sc, l_sc, g_sc, acc_sc):
    j = pl.program_id(1)
    n_blk = acc_sc.shape[0]
    stat_shape = m_sc.shape

    @pl.when(j == 0)
    def _():
        m_sc[...] = jnp.full(stat_shape, NEG, F32)
        l_sc[...] = jnp.zeros(stat_shape, F32)
        g_sc[...] = jnp.zeros(stat_shape, F32)

    qb = (q_ref[0] * (A_HD ** -0.5)).astype(BF16)
    s = jnp.concatenate(
        [lax.dot_general(qb, kr[0].astype(BF16), HEAD_BATCH, preferred_element_type=F32)
         for kr in (k0_ref, k1_ref)], axis=-1)
    m = jnp.max(s, axis=-1, keepdims=True)
    pw = jnp.exp(s - m)
    half = PAGE_SIZE
    acc_sc[j] = (lax.dot_general(pw[..., :half].astype(BF16), v0_ref[0].astype(BF16), HEAD_BATCH_NT,
                                 preferred_element_type=F32)
                 + lax.dot_general(pw[..., half:].astype(BF16), v1_ref[0].astype(BF16), HEAD_BATCH_NT,
                                   preferred_element_type=F32))
    col = lax.broadcasted_iota(I32, stat_shape, 2) == j
    m_sc[...] = jnp.where(col, m, m_sc[...])
    l_sc[...] = jnp.where(col, jnp.sum(pw, axis=-1, keepdims=True), l_sc[...])
    g_sc[...] = jnp.where(col, jnp.sum(s, axis=-1, keepdims=True), g_sc[...])

    @pl.when(j == n_blk - 1)
    def _():
        jidx = lax.broadcasted_iota(I32, stat_shape, 2)
        gate = jnp.where(jidx < n_blk, g_sc[...], -jnp.inf)
        sel = jnp.zeros(stat_shape, F32)
        for _ in range(min(MOBA_TOPK, n_blk)):
            gmax = jnp.max(gate, axis=-1, keepdims=True)
            first = jnp.min(jnp.where(gate == gmax, jidx, n_blk), axis=-1, keepdims=True)
            hit = jidx == first
            sel = jnp.where(hit, 1.0, sel)
            gate = jnp.where(hit, -jnp.inf, gate)
        so = lax.dot_general(qb, kn_ref[0].astype(BF16), HEAD_BATCH_NT, preferred_element_type=F32)
        t_q = lax.broadcasted_iota(I32, so.shape, 1)
        t_k = lax.broadcasted_iota(I32, so.shape, 2)
        so = jnp.where(t_k <= t_q, so, NEG)
        m_blk = jnp.where(sel > 0.0, m_sc[...], NEG)
        m_tot = jnp.maximum(jnp.max(so, axis=-1, keepdims=True), jnp.max(m_blk, axis=-1, keepdims=True))
        po = jnp.exp(so - m_tot)
        wgt = jnp.where(sel > 0.0, jnp.exp(m_blk - m_tot), 0.0)
        l_tot = jnp.sum(po, axis=-1, keepdims=True) + jnp.sum(wgt * l_sc[...], axis=-1, keepdims=True)
        acc = lax.dot_general(po.astype(BF16), vn_ref[0].astype(BF16), HEAD_BATCH, preferred_element_type=F32)
        for jb in range(n_blk):
            acc = acc + wgt[:, :, jb:jb + 1] * acc_sc[jb]
        o_ref[0] = acc / l_tot


def _moba_sample(q_h, k_h, v_h, cache_kt, cache_vt, page_table_flat):
    nb, nh, t_pad, hd = q_h.shape
    ppb = MOBA_BLOCK // PAGE_SIZE
    assert ppb == 2
    n_pages = page_table_flat.shape[0] // nb
    n_blk = n_pages // ppb
    lanes = -(-n_blk // 128) * 128
    seq_map = lambda b, j, pt: (b, 0, 0, 0)
    page_spec = lambda r: pl.BlockSpec((1, nh, hd, PAGE_SIZE),
                                       lambda b, j, pt: (pt[b * n_pages + j * ppb + r], 0, 0, 0))
    grid_spec = pltpu.PrefetchScalarGridSpec(
        num_scalar_prefetch=1,
        grid=(nb, n_blk),
        in_specs=[pl.BlockSpec((1, nh, t_pad, hd), seq_map),
                  pl.BlockSpec((1, nh, t_pad, hd), seq_map),
                  pl.BlockSpec((1, nh, t_pad, hd), seq_map),
                  page_spec(0), page_spec(1), page_spec(0), page_spec(1)],
        out_specs=pl.BlockSpec((1, nh, t_pad, hd), seq_map),
        scratch_shapes=[pltpu.VMEM((nh, t_pad, lanes), F32),
                        pltpu.VMEM((nh, t_pad, lanes), F32),
                        pltpu.VMEM((nh, t_pad, lanes), F32),
                        pltpu.VMEM((n_blk, nh, t_pad, hd), F32)],
    )
    return pl.pallas_call(
        _moba_sample_kernel,
        grid_spec=grid_spec,
        out_shape=jax.ShapeDtypeStruct((nb, nh, t_pad, hd), F32),
        compiler_params=_cparams(("parallel", "arbitrary")),
        name="moba_sample",
    )(page_table_flat, q_h, k_h, v_h, cache_kt, cache_kt, cache_vt, cache_vt)


def _top16_desc(vals, payload, n_rows):
    t = vals.shape[1]
    r16 = lax.broadcasted_iota(I32, (P_TOPK, t), 0)
    out_v = jnp.zeros((P_TOPK, t), F32)
    out_p = jnp.zeros((P_TOPK, t), I32)
    big = jnp.int32(2 ** 30)
    for r in range(P_TOPK):
        m = jnp.max(vals, axis=0, keepdims=True)
        pick = jnp.min(jnp.where(vals == m, payload, big), axis=0, keepdims=True)
        vals = jnp.where(payload == pick, -jnp.inf, vals)
        out_v = jnp.where(r16 == r, m, out_v)
        out_p = jnp.where(r16 == r, pick, out_p)
    return out_v, out_p


def _peer_topk_kernel(q_ref, keys_ref, pidx_ref, par_ref, g_ref):
    t = q_ref.shape[0]
    kio = lax.broadcasted_iota(I32, (P_NKEYS, t), 0)
    e_heads, g_heads = [], []
    for h in range(P_HEADS):
        sv, si = [], []
        for p in range(2):
            c0 = (h * 2 + p) * (P_NKEYS)
            scores = lax.dot_general(keys_ref[h, p], q_ref[:, c0:c0 + 128], NT_DIMS,
                                     precision=HIGHEST, preferred_element_type=F32)
            v, i = _top16_desc(scores, kio, P_NKEYS)
            sv.append(v)
            si.append(i)
        half = P_TOPK // 2
        cand = jnp.concatenate(
            [sv[0][0:1, :] + sv[1]]
            + [sv[0][i:i + 1, :] + sv[1][0:half, :] for i in range(1, half)]
            + [sv[0][half:, :] + sv[1][0:1, :]], axis=0)
        eid = jnp.concatenate(
            [si[0][0:1, :] * P_NKEYS + si[1]]
            + [si[0][i:i + 1, :] * P_NKEYS + si[1][0:half, :] for i in range(1, half)]
            + [si[0][half:, :] * P_NKEYS + si[1][0:1, :]], axis=0)
        cv, ce = _top16_desc(cand, eid, cand.shape[0])
        ex = jnp.exp(cv - cv[0:1, :])
        g_heads.append(ex / jnp.sum(ex, axis=0, keepdims=True))
        e_heads.append(ce)
    e_all = jnp.concatenate(e_heads, axis=0).T
    pidx_ref[...] = e_all >> 1
    par_ref[...] = (e_all & 1).astype(F32)
    g_ref[...] = jnp.concatenate(g_heads, axis=0).T


def _peer_topk(q, keys):
    nt = q.shape[0]
    t = TOK_BLOCK
    return pl.pallas_call(
        _peer_topk_kernel,
        grid=(nt // t,),
        in_specs=[pl.BlockSpec((t, q.shape[1]), lambda i: (i, 0)),
                  pl.BlockSpec(keys.shape, lambda i: (0, 0, 0, 0))],
        out_specs=[pl.BlockSpec((t, P_SLOTS), lambda i: (i, 0)),
                   pl.BlockSpec((t, P_SLOTS), lambda i: (i, 0)),
                   pl.BlockSpec((t, P_SLOTS), lambda i: (i, 0))],
        out_shape=[jax.ShapeDtypeStruct((nt, P_SLOTS), I32),
                   jax.ShapeDtypeStruct((nt, P_SLOTS), F32),
                   jax.ShapeDtypeStruct((nt, P_SLOTS), F32)],
        compiler_params=_cparams(("parallel",)),
        name="peer_topk",
    )(q, keys)


GATHER_CHUNK = 16
TILE_ROWS = 16
SLOT_COLS = P_SLOTS * TILE_ROWS


def _gather_tiles(tab_ref, pidx_ref, n, c):
    return jnp.concatenate(
        [tab_ref[pidx_ref[n, c * GATHER_CHUNK + i]] for i in range(GATHER_CHUNK)], axis=0)


def _split_bf16(x):
    hi = x.astype(BF16)
    lo = (x - hi.astype(F32)).astype(BF16)
    return jnp.concatenate([hi, lo], axis=0)


def _fold_row_mask(par_row, width):
    r = (lax.broadcasted_iota(I32, (8, width), 1) % TILE_ROWS).astype(F32)
    c = lax.broadcasted_iota(I32, (8, width), 0).astype(F32)
    return (r - 8.0 * par_row) == c


def _peer_u_kernel(pidx_ref, xf_ref, g_ref, par_ref, tab_ref, exp_ref, w_ref, parx_sc, hid_sc):
    tb = g_ref.shape[0]
    n_chunk = P_SLOTS // GATHER_CHUNK
    cw = GATHER_CHUNK * TILE_ROWS
    parx_sc[...] = jnp.dot(par_ref[...].astype(BF16), exp_ref[...], preferred_element_type=F32)

    def group(g8, carry):
        base = pl.multiple_of(g8 * 8, 8)
        par_blk = parx_sc[pl.ds(base, 8), :]
        rows = []
        for j in range(8):
            n = base + j
            x16 = _split_bf16(xf_ref[pl.ds(pl.multiple_of(n * 8, 8), 8), :])
            parts = []
            for c in range(n_chunk):
                y = lax.dot_general(x16, _gather_tiles(tab_ref, pidx_ref, n, c), NT_DIMS,
                                    preferred_element_type=F32)
                keep = _fold_row_mask(par_blk[j:j + 1, c * cw:(c + 1) * cw], cw)
                parts.append(jnp.sum(jnp.where(keep, y[0:8] + y[8:16], 0.0), axis=0, keepdims=True))
            rows.append(jnp.concatenate(parts, axis=1))
        s16 = _split_bf16(jnp.concatenate(rows, axis=0))
        h = lax.dot_general(s16, exp_ref[...], NT_DIMS, preferred_element_type=F32)
        hid_sc[pl.ds(base, 8), :] = h[0:8] + h[8:16]
        return carry

    lax.fori_loop(0, tb // 8, group, 0)
    hid = hid_sc[...]
    gelu = 0.5 * hid * (1.0 + lax.erf(hid * (2.0 ** -0.5)))
    w_ref[...] = g_ref[...] * gelu


def _peer_v_kernel(pidx_ref, w_ref, par_ref, tab_ref, exp_ref, o_ref, wx_sc, parx_sc):
    tb = w_ref.shape[0]
    n_chunk = P_SLOTS // GATHER_CHUNK
    cw = GATHER_CHUNK * TILE_ROWS
    parx_sc[...] = jnp.dot(par_ref[...].astype(BF16), exp_ref[...], preferred_element_type=F32)
    w = w_ref[...]
    w_hi = w.astype(BF16)
    w_lo = (w - w_hi.astype(F32)).astype(BF16)
    wx_sc[...] = (jnp.dot(w_hi, exp_ref[...], preferred_element_type=F32)
                  + jnp.dot(w_lo, exp_ref[...], preferred_element_type=F32))

    def group(g8, carry):
        base = pl.multiple_of(g8 * 8, 8)
        par_blk = parx_sc[pl.ds(base, 8), :]
        w_blk = wx_sc[pl.ds(base, 8), :]
        for j in range(8):
            n = base + j
            keep = _fold_row_mask(par_blk[j:j + 1, :], SLOT_COLS)
            lhs = _split_bf16(jnp.where(keep, w_blk[j:j + 1, :], 0.0))
            acc = jnp.zeros((16, 128), F32)
            for c in range(n_chunk):
                acc = acc + jnp.dot(lhs[:, c * cw:(c + 1) * cw], _gather_tiles(tab_ref, pidx_ref, n, c),
                                    preferred_element_type=F32)
            o_ref[pl.ds(pl.multiple_of(n * 8, 8), 8), :] = acc[0:8] + acc[8:16]
        return carry

    lax.fori_loop(0, tb // 8, group, 0)


def _peer_gather(x, pidx, par, g, u_pairs, v_pairs):
    nt, d = x.shape
    fold = d // 128
    tb = TOK_BLOCK
    n_pair = u_pairs.shape[0]
    tab_spec = pl.BlockSpec((n_pair, TILE_ROWS, 128), lambda i: (0, 0, 0), pipeline_mode=pl.Buffered(1))
    smem_spec = pl.BlockSpec((tb, P_SLOTS), lambda i: (i, 0), memory_space=pltpu.SMEM)
    slot_spec = pl.BlockSpec((tb, P_SLOTS), lambda i: (i, 0))
    exp_spec = pl.BlockSpec((P_SLOTS, SLOT_COLS), lambda i: (0, 0))
    expand = (jnp.arange(SLOT_COLS)[None, :] // TILE_ROWS == jnp.arange(P_SLOTS)[:, None]).astype(BF16)
    w = pl.pallas_call(
        _peer_u_kernel,
        grid=(nt // tb,),
        in_specs=[smem_spec, pl.BlockSpec((tb * fold, 128), lambda i: (i, 0)), slot_spec, slot_spec,
                  tab_spec, exp_spec],
        out_specs=slot_spec,
        out_shape=jax.ShapeDtypeStruct((nt, P_SLOTS), F32),
        scratch_shapes=[pltpu.VMEM((tb, SLOT_COLS), F32),
                        pltpu.VMEM((tb, P_SLOTS), F32)],
        compiler_params=_cparams(("parallel",)),
        name="peer_u",
    )(pidx, x.reshape(nt * fold, 128), g, par, u_pairs, expand)
    out = pl.pallas_call(
        _peer_v_kernel,
        grid=(nt // tb,),
        in_specs=[smem_spec, slot_spec, slot_spec, tab_spec, exp_spec],
        out_specs=pl.BlockSpec((tb * fold, 128), lambda i: (i, 0)),
        out_shape=jax.ShapeDtypeStruct((nt * fold, 128), F32),
        scratch_shapes=[pltpu.VMEM((tb, SLOT_COLS), F32),
                        pltpu.VMEM((tb, SLOT_COLS), F32)],
        compiler_params=_cparams(("parallel",)),
        name="peer_v",
    )(pidx, w, par, v_pairs, expand)
    return out.reshape(nt, d)


def _pair_table(tab):
    e, d = tab.shape
    return tab.astype(BF16).reshape(e // 2, 2 * (d // 128), 128)


def _peer_layer(x, w_q, keys, u_tab, v_tab, ln_g, ln_b):
    q = _matmul(x, w_q.astype(BF16))
    pidx, par, g = _peer_topk(q, keys)
    y = _peer_gather(x, pidx, par, g, _pair_table(u_tab), _pair_table(v_tab))
    return _res_ln(x, y, ln_g, ln_b)


def kernel(x_prompt, x_sample, state_mlstm_C, state_mlstm_n, state_mlstm_m, cache_k, cache_v, page_table,
           w_in_a, b_gate_a, mh_norm_a, w_out_a, w_kv, w_q_b, w_out_b,
           ln_mix_g, ln_mix_b, ln_ffn_g, ln_ffn_b, peer_wq, peer_keys, peer_u, peer_v):
    nb_p, seq_p, d = x_prompt.shape
    nb_s, seq_s, _ = x_sample.shape
    n_p, n_s = nb_p * seq_p, nb_s * seq_s
    nt = -(-(n_p + n_s) // ROW_TILE) * ROW_TILE
    past_len = page_table.shape[1] * PAGE_SIZE
    ppb = MOBA_BLOCK // PAGE_SIZE
    assert past_len % MOBA_BLOCK == 0 and seq_p % MOBA_BLOCK == 0 and n_p % ROW_TILE == 0
    assert state_mlstm_C.shape[0] == 1 and w_q_b.shape[0] == 1

    x = jnp.concatenate([x_prompt.reshape(n_p, d), x_sample.reshape(n_s, d),
                         jnp.zeros((nt - n_p - n_s, d), F32)], axis=0)

    hn, states = _mlstm_layer(x, n_p, nb_p, seq_p, nb_s, seq_s,
                              state_mlstm_C[0], state_mlstm_n[0], state_mlstm_m[0],
                              w_in_a[0], b_gate_a[0], mh_norm_a[0])
    x = _matmul_res_ln(hn, w_out_a[0].astype(BF16), x, ln_mix_g[0], ln_mix_b[0])
    x = _peer_layer(x, peer_wq[0], peer_keys[0], peer_u[0], peer_v[0], ln_ffn_g[0], ln_ffn_b[0])

    qkv = _matmul(x, jnp.concatenate([w_q_b[0], w_kv], axis=1).astype(BF16))
    pos = jnp.concatenate([jnp.tile(jnp.arange(seq_p), nb_p),
                           past_len + jnp.tile(jnp.arange(seq_s), nb_s),
                           jnp.zeros((nt - n_p - n_s,), I32)])
    q_rot, k_rot, kmean = _rope(qkv, _rope_tables(pos), d)
    attn_p = _moba_prompt(q_rot, k_rot, qkv, kmean.reshape(nt // MOBA_BLOCK, d), nb_p, seq_p, d)

    q_s = q_rot[n_p:n_p + n_s].reshape(nb_s, seq_s, d)
    k_s = k_rot[n_p:n_p + n_s].reshape(nb_s, seq_s, d)
    v_s = qkv[n_p:n_p + n_s, 2 * d:].reshape(nb_s, seq_s, d)
    pt_flat = page_table.reshape(-1).astype(I32)

    def head_major(a):
        a = a.reshape(nb_s, seq_s, A_HEADS, A_HD).transpose(0, 2, 1, 3)
        return jnp.pad(a, ((0, 0), (0, 0), (0, T_PAD - seq_s), (0, 0)))

    attn_s = _moba_sample(head_major(q_s), head_major(k_s), head_major(v_s),
                          cache_k.transpose(0, 2, 3, 1), cache_v.transpose(0, 2, 3, 1), pt_flat)
    attn_s = attn_s[:, :, :seq_s, :].transpose(0, 2, 1, 3)

    attn = jnp.concatenate([attn_p, attn_s.reshape(n_s, d), jnp.zeros((nt - n_p - n_s, d), F32)], axis=0)
    x = _matmul_res_ln(attn, w_out_b[0].astype(BF16), x, ln_mix_g[1], ln_mix_b[1])
    x = _peer_layer(x, peer_wq[1], peer_keys[1], peer_u[1], peer_v[1], ln_ffn_g[1], ln_ffn_b[1])

    y_prompt = x[:n_p].reshape(nb_p, seq_p, d)
    y_sample = x[n_p:n_p + n_s].reshape(nb_s, seq_s, d)
    c_p, n_pr, m_p, c_s, n_sm, m_s = states
    k_prompt = k_rot[:n_p].reshape(nb_p, seq_p, A_HEADS, A_HD)
    v_prompt = qkv[:n_p, 2 * d:].reshape(nb_p, seq_p, A_HEADS, A_HD)
    return (y_prompt, y_sample, c_p[None], n_pr[None], m_p[None], c_s[None], n_sm[None], m_s[None],
            k_prompt, v_prompt, k_s.reshape(nb_s, seq_s, A_HEADS, A_HD), v_s.reshape(nb_s, seq_s, A_HEADS, A_HD))
```

```python
import functools

import jax
import jax.numpy as jnp
from jax import lax
from jax.experimental import pallas as pl
from jax.experimental.pallas import tpu as pltpu

F32 = jnp.float32
BF16 = jnp.bfloat16
I32 = jnp.int32
HIGHEST = lax.Precision.HIGHEST

DEPTH = 2
DN_ALPHA = (2.0 * DEPTH) ** 0.25
LN_EPS = 1e-5

M_HEADS = 8
M_DK = 64
M_DV = 128
MLSTM_CHUNK = 256
MLSTM_SAMPLE_CHUNK = 128

A_HEADS = 16
A_HD = 64
ROT_DIM = 16
ROPE_THETA = 500000.0
MOBA_BLOCK = 256
MOBA_TOPK = 3
PAGE_SIZE = 128

P_HEADS = 8
P_NKEYS = 128
P_TOPK = 16
P_SLOTS = P_HEADS * P_TOPK

ROW_TILE = 512
TOK_BLOCK = 128
NEG = -1e30
VMEM_LIMIT = 56 * 1024 * 1024

NT_DIMS = (((1,), (1,)), ((), ()))


def _cparams(semantics, vmem=VMEM_LIMIT):
    return pltpu.CompilerParams(dimension_semantics=semantics, vmem_limit_bytes=vmem)


def _mm_kernel(x_ref, w_ref, o_ref):
    o_ref[...] = jnp.dot(x_ref[...].astype(BF16), w_ref[...], preferred_element_type=F32)


def _matmul(x, w_bf16, tm=ROW_TILE):
    m, k = x.shape
    n = w_bf16.shape[1]
    return pl.pallas_call(
        _mm_kernel,
        grid=(m // tm,),
        in_specs=[pl.BlockSpec((tm, k), lambda i: (i, 0)),
                  pl.BlockSpec((k, n), lambda i: (0, 0))],
        out_specs=pl.BlockSpec((tm, n), lambda i: (i, 0)),
        out_shape=jax.ShapeDtypeStruct((m, n), F32),
        compiler_params=_cparams(("parallel",)),
        name="matmul",
    )(x, w_bf16)


def _layer_norm_rows(z, g, b):
    mu = jnp.mean(z, axis=-1, keepdims=True)
    zc = z - mu
    var = jnp.mean(zc * zc, axis=-1, keepdims=True)
    return zc * lax.rsqrt(var + LN_EPS) * g + b


def _mm_ln_kernel(a_ref, w_ref, x_ref, g_ref, b_ref, o_ref):
    y = jnp.dot(a_ref[...].astype(BF16), w_ref[...], preferred_element_type=F32)
    o_ref[...] = _layer_norm_rows(DN_ALPHA * x_ref[...] + y, g_ref[...], b_ref[...])


def _matmul_res_ln(a, w_bf16, x_res, g, b, tm=ROW_TILE):
    m, k = a.shape
    n = w_bf16.shape[1]
    return pl.pallas_call(
        _mm_ln_kernel,
        grid=(m // tm,),
        in_specs=[pl.BlockSpec((tm, k), lambda i: (i, 0)),
                  pl.BlockSpec((k, n), lambda i: (0, 0)),
                  pl.BlockSpec((tm, n), lambda i: (i, 0)),
                  pl.BlockSpec((1, n), lambda i: (0, 0)),
                  pl.BlockSpec((1, n), lambda i: (0, 0))],
        out_specs=pl.BlockSpec((tm, n), lambda i: (i, 0)),
        out_shape=jax.ShapeDtypeStruct((m, n), F32),
        compiler_params=_cparams(("parallel",)),
        name="matmul_res_ln",
    )(a, w_bf16, x_res, g.reshape(1, n), b.reshape(1, n))


def _res_ln_kernel(x_ref, y_ref, g_ref, b_ref, o_ref):
    o_ref[...] = _layer_norm_rows(DN_ALPHA * x_ref[...] + y_ref[...], g_ref[...], b_ref[...])


def _res_ln(x, y, g, b, tm=ROW_TILE):
    m, n = x.shape
    return pl.pallas_call(
        _res_ln_kernel,
        grid=(m // tm,),
        in_specs=[pl.BlockSpec((tm, n), lambda i: (i, 0)),
                  pl.BlockSpec((tm, n), lambda i: (i, 0)),
                  pl.BlockSpec((1, n), lambda i: (0, 0)),
                  pl.BlockSpec((1, n), lambda i: (0, 0))],
        out_specs=pl.BlockSpec((tm, n), lambda i: (i, 0)),
        out_shape=jax.ShapeDtypeStruct((m, n), F32),
        compiler_params=_cparams(("parallel",)),
        name="res_ln",
    )(x, y, g.reshape(1, n), b.reshape(1, n))


def _gates_kernel(x_ref, w_ref, b_ref, o_ref):
    g = jnp.dot(x_ref[...], w_ref[...], precision=HIGHEST, preferred_element_type=F32) + b_ref[...]
    lane = lax.broadcasted_iota(I32, g.shape, 1)
    log_sig = jnp.minimum(g, 0.0) - jnp.log1p(jnp.exp(-jnp.abs(g)))
    o_ref[...] = jnp.where(lane < M_HEADS, g, log_sig)


def _mlstm_gates(x, w_gate, b_gate, tm=ROW_TILE):
    m, k = x.shape
    w = jnp.zeros((k, 128), F32).at[:, :2 * M_HEADS].set(w_gate)
    b = jnp.zeros((1, 128), F32).at[0, :2 * M_HEADS].set(b_gate)
    return pl.pallas_call(
        _gates_kernel,
        grid=(m // tm,),
        in_specs=[pl.BlockSpec((tm, k), lambda i: (i, 0)),
                  pl.BlockSpec((k, 128), lambda i: (0, 0)),
                  pl.BlockSpec((1, 128), lambda i: (0, 0))],
        out_specs=pl.BlockSpec((tm, 128), lambda i: (i, 0)),
        out_shape=jax.ShapeDtypeStruct((m, 128), F32),
        compiler_params=_cparams(("parallel",)),
        name="mlstm_gates",
    )(x, w, b)


def _mlstm_kernel(q_ref, kt_ref, v_ref, o_ref, gc_ref, gr_ref, c0_ref, m0_ref, gain_ref,
                  hn_ref, cout_ref, mout_ref, c_sc, m_sc):
    c = pl.program_id(1)
    L = q_ref.shape[0]

    @pl.when(c == 0)
    def _():
        c_sc[...] = c0_ref[0]
        m_sc[...] = m0_ref[0]

    gc = gc_ref[...]
    gr = gr_ref[0]
    row = lax.broadcasted_iota(I32, (L, L), 0)
    col = lax.broadcasted_iota(I32, (L, L), 1)
    causal = col <= row
    b_col = jnp.dot(causal.astype(F32), gc, precision=HIGHEST, preferred_element_type=F32)
    b_row = jnp.dot(gr, (row <= col).astype(F32), precision=HIGHEST, preferred_element_type=F32)
    ones_lane0 = (lax.broadcasted_iota(I32, (L, 128), 1) == 0).astype(BF16)

    for h in range(M_HEADS):
        bc = b_col[:, M_HEADS + h:M_HEADS + h + 1]
        br = b_row[M_HEADS + h:M_HEADS + h + 1, :]
        igr = gr[h:h + 1, :]
        m_prev = m_sc[h:h + 1, 0:1]
        dlog = jnp.where(causal, bc - br + igr, NEG)
        m_inter = bc + m_prev
        m_t = jnp.maximum(m_inter, jnp.max(dlog, axis=1, keepdims=True))
        w_intra = jnp.exp(dlog - m_t)
        w_inter = jnp.exp(m_inter - m_t)

        qh = (q_ref[:, h * M_DK:(h + 1) * M_DK] * (M_DK ** -0.5)).astype(BF16)
        kth = kt_ref[0, h * M_DK:(h + 1) * M_DK, :]
        v_ext = jnp.concatenate(
            [v_ref[:, h * M_DV:(h + 1) * M_DV].astype(BF16), ones_lane0], axis=1)
        s = jnp.dot(qh, kth.astype(BF16), preferred_element_type=F32) * w_intra
        c_h = c_sc[h]
        nd = (jnp.dot(s.astype(BF16), v_ext, preferred_element_type=F32)
              + w_inter * jnp.dot(qh, c_h.astype(BF16), preferred_element_type=F32))
        num = nd[:, :M_DV]
        den = nd[:, M_DV:M_DV + 1]
        hh = num / jnp.maximum(jnp.abs(den), jnp.exp(-m_t))
        hh = hh * lax.rsqrt(jnp.mean(hh * hh, axis=1, keepdims=True) + 1e-6)
        og = o_ref[:, h * M_DV:(h + 1) * M_DV]
        hn_ref[:, h * M_DV:(h + 1) * M_DV] = (
            hh * gain_ref[:, h * M_DV:(h + 1) * M_DV] * (1.0 / (1.0 + jnp.exp(-og))))

        m_new = m_t[L - 1:L, :]
        b_last = bc[L - 1:L, :]
        decay = jnp.exp(b_last + m_prev - m_new)
        w_s = jnp.exp(b_last - br + igr - m_new)
        kw = (kth * w_s).astype(BF16)
        c_sc[h] = decay * c_h + jnp.dot(kw, v_ext, preferred_element_type=F32)
        m_sc[h:h + 1, :] = jnp.broadcast_to(m_new, (1, 128))

    @pl.when(c == pl.num_programs(1) - 1)
    def _():
        cout_ref[0] = c_sc[...]
        mout_ref[0] = m_sc[...]


def _mlstm(proj, kt, gates, gates_t, c0_ext, m0_b, gain, nb, seq, chunk):
    nc = seq // chunk
    qk = M_HEADS * M_DK
    vd = M_HEADS * M_DV
    row_blk = lambda b, c: (b * nc + c, 0)
    return pl.pallas_call(
        _mlstm_kernel,
        grid=(nb, nc),
        in_specs=[pl.BlockSpec((chunk, qk), row_blk),
                  pl.BlockSpec((1, qk, chunk), lambda b, c: (b, 0, c)),
                  pl.BlockSpec((chunk, vd), lambda b, c: (b * nc + c, 1)),
                  pl.BlockSpec((chunk, vd), lambda b, c: (b * nc + c, 2)),
                  pl.BlockSpec((chunk, 128), row_blk),
                  pl.BlockSpec((1, 2 * M_HEADS, chunk), lambda b, c: (b, 0, c)),
                  pl.BlockSpec((1, M_HEADS, M_DK, 2 * M_DV), lambda b, c: (b, 0, 0, 0)),
                  pl.BlockSpec((1, M_HEADS, 128), lambda b, c: (b, 0, 0)),
                  pl.BlockSpec((1, vd), lambda b, c: (0, 0))],
        out_specs=[pl.BlockSpec((chunk, vd), row_blk),
                   pl.BlockSpec((1, M_HEADS, M_DK, 2 * M_DV), lambda b, c: (b, 0, 0, 0)),
                   pl.BlockSpec((1, M_HEADS, 128), lambda b, c: (b, 0, 0))],
        out_shape=[jax.ShapeDtypeStruct((nb * seq, vd), F32),
                   jax.ShapeDtypeStruct((nb, M_HEADS, M_DK, 2 * M_DV), F32),
                   jax.ShapeDtypeStruct((nb, M_HEADS, 128), F32)],
        scratch_shapes=[pltpu.VMEM((M_HEADS, M_DK, 2 * M_DV), F32),
                        pltpu.VMEM((M_HEADS, 128), F32)],
        compiler_params=_cparams(("parallel", "arbitrary")),
        name="mlstm",
    )(proj, kt, proj, proj, gates, gates_t, c0_ext, m0_b, gain.reshape(1, vd))


def _mlstm_layer(x_all, n_p, nb_p, seq_p, nb_s, seq_s, st_c, st_n, st_m, w_in, b_gate, gain):
    nt = x_all.shape[0]
    qk = M_HEADS * M_DK
    vd = M_HEADS * M_DV
    n_s = nb_s * seq_s
    proj = _matmul(x_all, w_in[:, :2 * qk + 2 * vd].astype(BF16))
    gates = _mlstm_gates(x_all, w_in[:, 2 * qk + 2 * vd:], b_gate)

    kt_p = proj[:n_p, qk:2 * qk].reshape(nb_p, seq_p, qk).transpose(0, 2, 1)
    gt_p = gates[:n_p, :2 * M_HEADS].reshape(nb_p, seq_p, 2 * M_HEADS).transpose(0, 2, 1)
    c0_p = jnp.zeros((nb_p, M_HEADS, M_DK, 2 * M_DV), F32)
    m0_p = jnp.zeros((nb_p, M_HEADS, 128), F32)
    hn_p, c_p, m_p = _mlstm(proj, kt_p, gates, gt_p, c0_p, m0_p, gain, nb_p, seq_p,
                            min(MLSTM_CHUNK, seq_p))

    ls = MLSTM_SAMPLE_CHUNK
    pad = ((0, 0), (0, ls - seq_s), (0, 0))
    proj_s = jnp.pad(proj[n_p:n_p + n_s].reshape(nb_s, seq_s, -1), pad)
    g_s = gates[n_p:n_p + n_s].reshape(nb_s, seq_s, 128)
    pad_row = jnp.where(jnp.arange(128) < M_HEADS, NEG, 0.0).astype(F32)
    g_s = jnp.concatenate([g_s, jnp.broadcast_to(pad_row, (nb_s, ls - seq_s, 128))], axis=1)
    kt_s = proj_s[:, :, qk:2 * qk].transpose(0, 2, 1)
    gt_s = g_s[:, :, :2 * M_HEADS].transpose(0, 2, 1)
    c0_s = jnp.concatenate([st_c, st_n[..., None],
                            jnp.zeros(st_c.shape[:3] + (M_DV - 1,), F32)], axis=-1)
    m0_s = jnp.broadcast_to(st_m[..., None], st_m.shape + (128,))
    hn_s, c_s, m_s = _mlstm(proj_s.reshape(nb_s * ls, -1), kt_s, g_s.reshape(nb_s * ls, 128), gt_s,
                            c0_s, m0_s, gain, nb_s, ls, ls)
    hn_s = hn_s.reshape(nb_s, ls, vd)[:, :seq_s].reshape(n_s, vd)
    hn = jnp.concatenate([hn_p, hn_s, jnp.zeros((nt - n_p - n_s, vd), F32)], axis=0)
    states = (c_p[..., :M_DV], c_p[..., M_DV], m_p[..., 0],
              c_s[..., :M_DV], c_s[..., M_DV], m_s[..., 0])
    return hn, states


def _rope_kernel(q_ref, k_ref, ca_ref, cb_ref, cc_ref, qo_ref, ko_ref, km_ref):
    ca, cb, cc = ca_ref[...], cb_ref[...], cc_ref[...]
    half = ROT_DIM // 2
    for j in range(q_ref.shape[1] // 128):
        sl = slice(j * 128, (j + 1) * 128)
        for src, dst in ((q_ref, qo_ref), (k_ref, ko_ref)):
            x = src[:, sl]
            dst[:, sl] = (x * ca + pltpu.roll(x, 128 - half, 1) * cb + pltpu.roll(x, half, 1) * cc)
    km_ref[0] = jnp.mean(ko_ref[...], axis=0, keepdims=True)


def _rope_tables(pos):
    half = ROT_DIM // 2
    inv = ROPE_THETA ** (-jnp.arange(0, ROT_DIM, 2, dtype=F32) / ROT_DIM)
    ang = pos.astype(F32)[:, None] * inv[None, :]
    cos, sin = jnp.cos(ang), jnp.sin(ang)
    lane = jnp.arange(128) % A_HD
    f = lane % half
    in_lo = lane < half
    in_hi = (lane >= half) & (lane < ROT_DIM)
    ca = jnp.where((in_lo | in_hi)[None, :], cos[:, f], 1.0)
    cb = jnp.where(in_lo[None, :], -sin[:, f], 0.0)
    cc = jnp.where(in_hi[None, :], sin[:, f], 0.0)
    return ca.astype(F32), cb.astype(F32), cc.astype(F32)


def _rope(qkv, tables, d):
    nt = qkv.shape[0]
    tm = MOBA_BLOCK
    spec_t = pl.BlockSpec((tm, 128), lambda i: (i, 0))
    return pl.pallas_call(
        _rope_kernel,
        grid=(nt // tm,),
        in_specs=[pl.BlockSpec((tm, d), lambda i: (i, 0)),
                  pl.BlockSpec((tm, d), lambda i: (i, 1)),
                  spec_t, spec_t, spec_t],
        out_specs=[pl.BlockSpec((tm, d), lambda i: (i, 0)),
                   pl.BlockSpec((tm, d), lambda i: (i, 0)),
                   pl.BlockSpec((1, 1, d), lambda i: (i, 0, 0))],
        out_shape=[jax.ShapeDtypeStruct((nt, d), F32),
                   jax.ShapeDtypeStruct((nt, d), F32),
                   jax.ShapeDtypeStruct((nt // tm, 1, d), F32)],
        compiler_params=_cparams(("parallel",)),
        name="rope",
    )(qkv, qkv, *tables)


def _moba_prompt_kernel(q_ref, k_ref, vt_ref, km_ref, o_ref):
    qi = pl.program_id(2)
    blk = MOBA_BLOCK
    nblk = km_ref.shape[0]
    q = q_ref[...] * (A_HD ** -0.5)
    lane = lax.broadcasted_iota(I32, (blk, 128), 1)
    key_i = lax.broadcasted_iota(I32, (blk, blk), 0)
    qry_i = lax.broadcasted_iota(I32, (blk, blk), 1)
    jrow = lax.broadcasted_iota(I32, (nblk, blk), 0)
    start = pl.multiple_of(qi * blk, blk)
    k_own = k_ref[pl.ds(start, blk), :].astype(BF16)
    vt_own = vt_ref[qi].astype(BF16)

    qbs, sels, init = [], [], []
    for hh in range(2):
        qh = jnp.where((lane // A_HD) == hh, q, 0.0)
        qb = qh.astype(BF16)
        gate = lax.dot_general(km_ref[...], qh, NT_DIMS, precision=HIGHEST,
                               preferred_element_type=F32)
        valid = jrow < qi
        gm = jnp.where(valid, gate, NEG)
        rank = jnp.zeros((nblk, blk), I32)
        for jp in range(nblk - 1):
            gj = gm[jp:jp + 1, :]
            beats = (gj > gm) | ((gj == gm) & (jp < jrow))
            rank = rank + jnp.where(beats, 1, 0)
        sels.append(jnp.where(valid & (rank < MOBA_TOPK), 1.0, 0.0))
        qbs.append(qb)

        s = lax.dot_general(k_own, qb, NT_DIMS, preferred_element_type=F32)
        s = jnp.where(key_i <= qry_i, s, NEG)
        m0 = jnp.max(s, axis=0, keepdims=True)
        p = jnp.exp(s - m0)
        init += [m0, jnp.sum(p, axis=0, keepdims=True),
                 jnp.dot(vt_own, p.astype(BF16), preferred_element_type=F32)]

    def scores(j):
        kj = k_ref[pl.ds(pl.multiple_of(j * blk, blk), blk), :].astype(BF16)
        return [lax.dot_general(kj, qbs[hh], NT_DIMS, preferred_element_type=F32) for hh in range(2)]

    def body(j, carry):
        s_next = scores(jnp.minimum(j + 1, qi))
        vtj = vt_ref[j].astype(BF16)
        new = []
        for hh in range(2):
            m, l, acc = carry[3 * hh:3 * hh + 3]
            selj = jnp.sum(jnp.where(jrow == j, sels[hh], 0.0), axis=0, keepdims=True)
            sj = jnp.where(selj > 0.0, carry[6 + hh], NEG)
            m_new = jnp.maximum(m, jnp.max(sj, axis=0, keepdims=True))
            alpha = jnp.exp(m - m_new)
            pj = jnp.exp(sj - m_new)
            new += [m_new, alpha * l + jnp.sum(pj, axis=0, keepdims=True),
                    alpha * acc + jnp.dot(vtj, pj.astype(BF16), preferred_element_type=F32)]
        return tuple(new + s_next)

    fin = lax.fori_loop(0, qi, body, tuple(init + scores(0)))
    drow = lax.broadcasted_iota(I32, (128, blk), 0)
    out_t = jnp.where(drow < A_HD, fin[2] / fin[1], fin[5] / fin[4])
    o_ref[...] = out_t.T


def _moba_prompt(q_rot, k_rot, qkv, kmean, nb, seq, d):
    nq = seq // MOBA_BLOCK
    ng = d // 128
    v_t = qkv[:nb * seq, 2 * d:].reshape(nb * nq, MOBA_BLOCK, d).transpose(0, 2, 1)
    return pl.pallas_call(
        _moba_prompt_kernel,
        grid=(nb, ng, nq),
        in_specs=[pl.BlockSpec((MOBA_BLOCK, 128), lambda b, g, i: (b * nq + i, g)),
                  pl.BlockSpec((seq, 128), lambda b, g, i: (b, g)),
                  pl.BlockSpec((nq, 128, MOBA_BLOCK), lambda b, g, i: (b, g, 0)),
                  pl.BlockSpec((nq, 128), lambda b, g, i: (b, g))],
        out_specs=pl.BlockSpec((MOBA_BLOCK, 128), lambda b, g, i: (b * nq + i, g)),
        out_shape=jax.ShapeDtypeStruct((nb * seq, d), F32),
        compiler_params=_cparams(("parallel", "parallel", "arbitrary")),
        name="moba_prompt",
    )(q_rot, k_rot, v_t, kmean)


HEAD_BATCH = (((2,), (1,)), ((0,), (0,)))
HEAD_BATCH_NT = (((2,), (2,)), ((0,), (0,)))
T_PAD = 8


def _moba_sample_kernel(pt_ref, q_ref, kn_ref, vn_ref, k0_ref, k1_ref, v0_ref, v1_ref, o_ref,
                        m_sc, l_sc, g_sc, acc_sc):
    j = pl.program_id(1)
    n_blk = acc_sc.shape[0]
    stat_shape = m_sc.shape

    @pl.when(j == 0)
    def _():
        m_sc[...] = jnp.full(stat_shape, NEG, F32)
        l_sc[...] = jnp.zeros(stat_shape, F32)
        g_sc[...] = jnp.zeros(stat_shape, F32)

    qb = (q_ref[0] * (A_HD ** -0.5)).astype(BF16)
    s = jnp.concatenate(
        [lax.dot_general(qb, kr[0].astype(BF16), HEAD_BATCH, preferred_element_type=F32)
         for kr in (k0_ref, k1_ref)], axis=-1)
    m = jnp.max(s, axis=-1, keepdims=True)
    pw = jnp.exp(s - m)
    half = PAGE_SIZE
    acc_sc[j] = (lax.dot_general(pw[..., :half].astype(BF16), v0_ref[0].astype(BF16), HEAD_BATCH_NT,
                                 preferred_element_type=F32)
                 + lax.dot_general(pw[..., half:].astype(BF16), v1_ref[0].astype(BF16), HEAD_BATCH_NT,
                                   preferred_element_type=F32))
    col = lax.broadcasted_iota(I32, stat_shape, 2) == j
    m_sc[...] = jnp.where(col, m, m_sc[...])
    l_sc[...] = jnp.where(col, jnp.sum(pw, axis=-1, keepdims=True), l_sc[...])
    g_sc[...] = jnp.where(col, jnp.sum(s, axis=-1, keepdims=True), g_sc[...])

    @pl.when(j == n_blk - 1)
    def _():
        jidx = lax.broadcasted_iota(I32, stat_shape, 2)
        gate = jnp.where(jidx < n_blk, g_sc[...], -jnp.inf)
        sel = jnp.zeros(stat_shape, F32)
        for _ in range(min(MOBA_TOPK, n_blk)):
            gmax = jnp.max(gate, axis=-1, keepdims=True)
            first = jnp.min(jnp.where(gate == gmax, jidx, n_blk), axis=-1, keepdims=True)
            hit = jidx == first
            sel = jnp.where(hit, 1.0, sel)
            gate = jnp.where(hit, -jnp.inf, gate)
        so = lax.dot_general(qb, kn_ref[0].astype(BF16), HEAD_BATCH_NT, preferred_element_type=F32)
        t_q = lax.broadcasted_iota(I32, so.shape, 1)
        t_k = lax.broadcasted_iota(I32, so.shape, 2)
        so = jnp.where(t_k <= t_q, so, NEG)
        m_blk = jnp.where(sel > 0.0, m_sc[...], NEG)
        m_tot = jnp.maximum(jnp.max(so, axis=-1, keepdims=True), jnp.max(m_blk, axis=-1, keepdims=True))
        po = jnp.exp(so - m_tot)
        wgt = jnp.where(sel > 0.0, jnp.exp(m_blk - m_tot), 0.0)
        l_tot = jnp.sum(po, axis=-1, keepdims=True) + jnp.sum(wgt * l_sc[...], axis=-1, keepdims=True)
        acc = lax.dot_general(po.astype(BF16), vn_ref[0].astype(BF16), HEAD_BATCH, preferred_element_type=F32)
        for jb in range(n_blk):
            acc = acc + wgt[:, :, jb:jb + 1] * acc_sc[jb]
        o_ref[0] = acc / l_tot


def _moba_sample(q_h, k_h, v_h, cache_kt, cache_vt, page_table_flat):
    nb, nh, t_pad, hd = q_h.shape
    ppb = MOBA_BLOCK // PAGE_SIZE
    assert ppb == 2
    n_pages = page_table_flat.shape[0] // nb
    n_blk = n_pages // ppb
    lanes = -(-n_blk // 128) * 128
    seq_map = lambda b, j, pt: (b, 0, 0, 0)
    page_spec = lambda r: pl.BlockSpec((1, nh, hd, PAGE_SIZE),
                                       lambda b, j, pt: (pt[b * n_pages + j * ppb + r], 0, 0, 0))
    grid_spec = pltpu.PrefetchScalarGridSpec(
        num_scalar_prefetch=1,
        grid=(nb, n_blk),
        in_specs=[pl.BlockSpec((1, nh, t_pad, hd), seq_map),
                  pl.BlockSpec((1, nh, t_pad, hd), seq_map),
                  pl.BlockSpec((1, nh, t_pad, hd), seq_map),
                  page_spec(0), page_spec(1), page_spec(0), page_spec(1)],
        out_specs=pl.BlockSpec((1, nh, t_pad, hd), seq_map),
        scratch_shapes=[pltpu.VMEM((nh, t_pad, lanes), F32),
                        pltpu.VMEM((nh, t_pad, lanes), F32),
                        pltpu.VMEM((nh, t_pad, lanes), F32),
                        pltpu.VMEM((n_blk, nh, t_pad, hd), F32)],
    )
    return pl.pallas_call(
        _moba_sample_kernel,
        grid_spec=grid_spec,
        out_shape=jax.ShapeDtypeStruct((nb, nh, t_pad, hd), F32),
        compiler_params=_cparams(("parallel", "arbitrary")),
        name="moba_sample",
    )(page_table_flat, q_h, k_h, v_h, cache_kt, cache_kt, cache_vt, cache_vt)


def _top16_desc(vals, payload, n_rows):
    t = vals.shape[1]
    r16 = lax.broadcasted_iota(I32, (P_TOPK, t), 0)
    out_v = jnp.zeros((P_TOPK, t), F32)
    out_p = jnp.zeros((P_TOPK, t), I32)
    big = jnp.int32(2 ** 30)
    for r in range(P_TOPK):
        m = jnp.max(vals, axis=0, keepdims=True)
        pick = jnp.min(jnp.where(vals == m, payload, big), axis=0, keepdims=True)
        vals = jnp.where(payload == pick, -jnp.inf, vals)
        out_v = jnp.where(r16 == r, m, out_v)
        out_p = jnp.where(r16 == r, pick, out_p)
    return out_v, out_p


def _peer_topk_kernel(q_ref, keys_ref, pidx_ref, par_ref, g_ref):
    t = q_ref.shape[0]
    kio = lax.broadcasted_iota(I32, (P_NKEYS, t), 0)
    e_heads, g_heads = [], []
    for h in range(P_HEADS):
        sv, si = [], []
        for p in range(2):
            c0 = (h * 2 + p) * (P_NKEYS)
            scores = lax.dot_general(keys_ref[h, p], q_ref[:, c0:c0 + 128], NT_DIMS,
                                     precision=HIGHEST, preferred_element_type=F32)
            v, i = _top16_desc(scores, kio, P_NKEYS)
            sv.append(v)
            si.append(i)
        half = P_TOPK // 2
        cand = jnp.concatenate(
            [sv[0][0:1, :] + sv[1]]
            + [sv[0][i:i + 1, :] + sv[1][0:half, :] for i in range(1, half)]
            + [sv[0][half:, :] + sv[1][0:1, :]], axis=0)
        eid = jnp.concatenate(
            [si[0][0:1, :] * P_NKEYS + si[1]]
            + [si[0][i:i + 1, :] * P_NKEYS + si[1][0:half, :] for i in range(1, half)]
            + [si[0][half:, :] * P_NKEYS + si[1][0:1, :]], axis=0)
        cv, ce = _top16_desc(cand, eid, cand.shape[0])
        ex = jnp.exp(cv - cv[0:1, :])
        g_heads.append(ex / jnp.sum(ex, axis=0, keepdims=True))
        e_heads.append(ce)
    e_all = jnp.concatenate(e_heads, axis=0).T
    pidx_ref[...] = e_all >> 1
    par_ref[...] = (e_all & 1).astype(F32)
    g_ref[...] = jnp.concatenate(g_heads, axis=0).T


def _peer_topk(q, keys):
    nt = q.shape[0]
    t = TOK_BLOCK
    return pl.pallas_call(
        _peer_topk_kernel,
        grid=(nt // t,),
        in_specs=[pl.BlockSpec((t, q.shape[1]), lambda i: (i, 0)),
                  pl.BlockSpec(keys.shape, lambda i: (0, 0, 0, 0))],
        out_specs=[pl.BlockSpec((t, P_SLOTS), lambda i: (i, 0)),
                   pl.BlockSpec((t, P_SLOTS), lambda i: (i, 0)),
                   pl.BlockSpec((t, P_SLOTS), lambda i: (i, 0))],
        out_shape=[jax.ShapeDtypeStruct((nt, P_SLOTS), I32),
                   jax.ShapeDtypeStruct((nt, P_SLOTS), F32),
                   jax.ShapeDtypeStruct((nt, P_SLOTS), F32)],
        compiler_params=_cparams(("parallel",)),
        name="peer_topk",
    )(q, keys)


GATHER_CHUNK = 16
TILE_ROWS = 16
SLOT_COLS = P_SLOTS * TILE_ROWS


def _gather_tiles(tab_ref, idx_smem, slot, j, c):
    return jnp.concatenate(
        [tab_ref[idx_smem[slot, j, c * GATHER_CHUNK + i]] for i in range(GATHER_CHUNK)], axis=0)


def _index_copy(pidx_ref, idx_smem, sem, g8, slot):
    return pltpu.make_async_copy(pidx_ref.at[pl.ds(pl.multiple_of(g8 * 8, 8), 8), :],
                                 idx_smem.at[slot], sem.at[slot])


def _for_token_groups(pidx_ref, idx_smem, sem, n_groups, body):
    _index_copy(pidx_ref, idx_smem, sem, 0, 0).start()

    def pair(gp, carry):
        for slot in range(2):
            g8 = gp * 2 + slot
            _index_copy(pidx_ref, idx_smem, sem, g8, slot).wait()

            @pl.when(g8 + 1 < n_groups)
            def _():
                _index_copy(pidx_ref, idx_smem, sem, g8 + 1, 1 - slot).start()

            body(g8, slot)
        return carry

    lax.fori_loop(0, n_groups // 2, pair, 0)


def _split_bf16(x):
    hi = x.astype(BF16)
    lo = (x - hi.astype(F32)).astype(BF16)
    return jnp.concatenate([hi, lo], axis=0)


def _fold_row_mask(par_row, width):
    r = (lax.broadcasted_iota(I32, (8, width), 1) % TILE_ROWS).astype(F32)
    c = lax.broadcasted_iota(I32, (8, width), 0).astype(F32)
    return (r - 8.0 * par_row) == c


def _peer_u_kernel(pidx_ref, xf_ref, g_ref, par_ref, tab_ref, exp_ref, w_ref, parx_sc, hid_sc,
                   idx_smem, sem):
    tb = g_ref.shape[0]
    n_chunk = P_SLOTS // GATHER_CHUNK
    cw = GATHER_CHUNK * TILE_ROWS
    parx_sc[...] = jnp.dot(par_ref[...].astype(BF16), exp_ref[...], preferred_element_type=F32)

    def group(g8, slot):
        base = pl.multiple_of(g8 * 8, 8)
        par_blk = parx_sc[pl.ds(base, 8), :]
        rows = []
        for j in range(8):
            n = base + j
            x16 = _split_bf16(xf_ref[pl.ds(pl.multiple_of(n * 8, 8), 8), :])
            parts = []
            for c in range(n_chunk):
                y = lax.dot_general(x16, _gather_tiles(tab_ref, idx_smem, slot, j, c), NT_DIMS,
                                    preferred_element_type=F32)
                keep = _fold_row_mask(par_blk[j:j + 1, c * cw:(c + 1) * cw], cw)
                parts.append(jnp.sum(jnp.where(keep, y[0:8] + y[8:16], 0.0), axis=0, keepdims=True))
            rows.append(jnp.concatenate(parts, axis=1))
        s16 = _split_bf16(jnp.concatenate(rows, axis=0))
        h = lax.dot_general(s16, exp_ref[...], NT_DIMS, preferred_element_type=F32)
        hid_sc[pl.ds(base, 8), :] = h[0:8] + h[8:16]

    _for_token_groups(pidx_ref, idx_smem, sem, tb // 8, group)
    hid = hid_sc[...]
    gelu = 0.5 * hid * (1.0 + lax.erf(hid * (2.0 ** -0.5)))
    w_ref[...] = g_ref[...] * gelu


def _peer_v_kernel(pidx_ref, w_ref, par_ref, tab_ref, exp_ref, o_ref, wx_sc, parx_sc, idx_smem, sem):
    tb = w_ref.shape[0]
    n_chunk = P_SLOTS // GATHER_CHUNK
    cw = GATHER_CHUNK * TILE_ROWS
    parx_sc[...] = jnp.dot(par_ref[...].astype(BF16), exp_ref[...], preferred_element_type=F32)
    w = w_ref[...]
    w_hi = w.astype(BF16)
    w_lo = (w - w_hi.astype(F32)).astype(BF16)
    wx_sc[...] = (jnp.dot(w_hi, exp_ref[...], preferred_element_type=F32)
                  + jnp.dot(w_lo, exp_ref[...], preferred_element_type=F32))

    def group(g8, slot):
        base = pl.multiple_of(g8 * 8, 8)
        par_blk = parx_sc[pl.ds(base, 8), :]
        w_blk = wx_sc[pl.ds(base, 8), :]
        for j in range(8):
            n = base + j
            keep = _fold_row_mask(par_blk[j:j + 1, :], SLOT_COLS)
            lhs = _split_bf16(jnp.where(keep, w_blk[j:j + 1, :], 0.0))
            acc = jnp.zeros((16, 128), F32)
            for c in range(n_chunk):
                acc = acc + jnp.dot(lhs[:, c * cw:(c + 1) * cw], _gather_tiles(tab_ref, idx_smem, slot, j, c),
                                    preferred_element_type=F32)
            o_ref[pl.ds(pl.multiple_of(n * 8, 8), 8), :] = acc[0:8] + acc[8:16]

    _for_token_groups(pidx_ref, idx_smem, sem, tb // 8, group)


def _peer_gather(x, pidx, par, g, u_pairs, v_pairs):
    nt, d = x.shape
    fold = d // 128
    tb = TOK_BLOCK
    n_pair = u_pairs.shape[0]
    tab_spec = pl.BlockSpec((n_pair, TILE_ROWS, 128), lambda i: (0, 0, 0), pipeline_mode=pl.Buffered(1))
    slot_spec = pl.BlockSpec((tb, P_SLOTS), lambda i: (i, 0))
    idx_staging = [pltpu.SMEM((2, 8, P_SLOTS), I32), pltpu.SemaphoreType.DMA((2,))]
    exp_spec = pl.BlockSpec((P_SLOTS, SLOT_COLS), lambda i: (0, 0))
    expand = (jnp.arange(SLOT_COLS)[None, :] // TILE_ROWS == jnp.arange(P_SLOTS)[:, None]).astype(BF16)
    w = pl.pallas_call(
        _peer_u_kernel,
        grid=(nt // tb,),
        in_specs=[slot_spec, pl.BlockSpec((tb * fold, 128), lambda i: (i, 0)), slot_spec, slot_spec,
                  tab_spec, exp_spec],
        out_specs=slot_spec,
        out_shape=jax.ShapeDtypeStruct((nt, P_SLOTS), F32),
        scratch_shapes=[pltpu.VMEM((tb, SLOT_COLS), F32),
                        pltpu.VMEM((tb, P_SLOTS), F32)] + idx_staging,
        compiler_params=_cparams(("parallel",)),
        name="peer_u",
    )(pidx, x.reshape(nt * fold, 128), g, par, u_pairs, expand)
    out = pl.pallas_call(
        _peer_v_kernel,
        grid=(nt // tb,),
        in_specs=[slot_spec, slot_spec, slot_spec, tab_spec, exp_spec],
        out_specs=pl.BlockSpec((tb * fold, 128), lambda i: (i, 0)),
        out_shape=jax.ShapeDtypeStruct((nt * fold, 128), F32),
        scratch_shapes=[pltpu.VMEM((tb, SLOT_COLS), F32),
                        pltpu.VMEM((tb, SLOT_COLS), F32)] + idx_staging,
        compiler_params=_cparams(("parallel",)),
        name="peer_v",
    )(pidx, w, par, v_pairs, expand)
    return out.reshape(nt, d)


def _pair_table(tab):
    e, d = tab.shape
    return tab.astype(BF16).reshape(e // 2, 2 * (d // 128), 128)


def _peer_layer(x, w_q, keys, u_tab, v_tab, ln_g, ln_b):
    q = _matmul(x, w_q.astype(BF16))
    pidx, par, g = _peer_topk(q, keys)
    y = _peer_gather(x, pidx, par, g, _pair_table(u_tab), _pair_table(v_tab))
    return _res_ln(x, y, ln_g, ln_b)


def kernel(x_prompt, x_sample, state_mlstm_C, state_mlstm_n, state_mlstm_m, cache_k, cache_v, page_table,
           w_in_a, b_gate_a, mh_norm_a, w_out_a, w_kv, w_q_b, w_out_b,
           ln_mix_g, ln_mix_b, ln_ffn_g, ln_ffn_b, peer_wq, peer_keys, peer_u, peer_v):
    nb_p, seq_p, d = x_prompt.shape
    nb_s, seq_s, _ = x_sample.shape
    n_p, n_s = nb_p * seq_p, nb_s * seq_s
    nt = -(-(n_p + n_s) // ROW_TILE) * ROW_TILE
    past_len = page_table.shape[1] * PAGE_SIZE
    ppb = MOBA_BLOCK // PAGE_SIZE
    assert past_len % MOBA_BLOCK == 0 and seq_p % MOBA_BLOCK == 0 and n_p % ROW_TILE == 0
    assert state_mlstm_C.shape[0] == 1 and w_q_b.shape[0] == 1

    x = jnp.concatenate([x_prompt.reshape(n_p, d), x_sample.reshape(n_s, d),
                         jnp.zeros((nt - n_p - n_s, d), F32)], axis=0)

    hn, states = _mlstm_layer(x, n_p, nb_p, seq_p, nb_s, seq_s,
                              state_mlstm_C[0], state_mlstm_n[0], state_mlstm_m[0],
                              w_in_a[0], b_gate_a[0], mh_norm_a[0])
    x = _matmul_res_ln(hn, w_out_a[0].astype(BF16), x, ln_mix_g[0], ln_mix_b[0])
    x = _peer_layer(x, peer_wq[0], peer_keys[0], peer_u[0], peer_v[0], ln_ffn_g[0], ln_ffn_b[0])

    qkv = _matmul(x, jnp.concatenate([w_q_b[0], w_kv], axis=1).astype(BF16))
    pos = jnp.concatenate([jnp.tile(jnp.arange(seq_p), nb_p),
                           past_len + jnp.tile(jnp.arange(seq_s), nb_s),
                           jnp.zeros((nt - n_p - n_s,), I32)])
    q_rot, k_rot, kmean = _rope(qkv, _rope_tables(pos), d)
    attn_p = _moba_prompt(q_rot, k_rot, qkv, kmean.reshape(nt // MOBA_BLOCK, d), nb_p, seq_p, d)

    q_s = q_rot[n_p:n_p + n_s].reshape(nb_s, seq_s, d)
    k_s = k_rot[n_p:n_p + n_s].reshape(nb_s, seq_s, d)
    v_s = qkv[n_p:n_p + n_s, 2 * d:].reshape(nb_s, seq_s, d)
    pt_flat = page_table.reshape(-1).astype(I32)

    def head_major(a):
        a = a.reshape(nb_s, seq_s, A_HEADS, A_HD).transpose(0, 2, 1, 3)
        return jnp.pad(a, ((0, 0), (0, 0), (0, T_PAD - seq_s), (0, 0)))

    attn_s = _moba_sample(head_major(q_s), head_major(k_s), head_major(v_s),
                          cache_k.transpose(0, 2, 3, 1), cache_v.transpose(0, 2, 3, 1), pt_flat)
    attn_s = attn_s[:, :, :seq_s, :].transpose(0, 2, 1, 3)

    attn = jnp.concatenate([attn_p, attn_s.reshape(n_s, d), jnp.zeros((nt - n_p - n_s, d), F32)], axis=0)
    x = _matmul_res_ln(attn, w_out_b[0].astype(BF16), x, ln_mix_g[1], ln_mix_b[1])
    x = _peer_layer(x, peer_wq[1], peer_keys[1], peer_u[1], peer_v[1], ln_ffn_g[1], ln_ffn_b[1])

    y_prompt = x[:n_p].reshape(nb_p, seq_p, d)
    y_sample = x[n_p:n_p + n_s].reshape(nb_s, seq_s, d)
    c_p, n_pr, m_p, c_s, n_sm, m_s = states
    k_prompt = k_rot[:n_p].reshape(nb_p, seq_p, A_HEADS, A_HD)
    v_prompt = qkv[:n_p, 2 * d:].reshape(nb_p, seq_p, A_HEADS, A_HD)
    return (y_prompt, y_sample, c_p[None], n_pr[None], m_p[None], c_s[None], n_sm[None], m_s[None],
            k_prompt, v_prompt, k_s.reshape(nb_s, seq_s, A_HEADS, A_HD), v_s.reshape(nb_s, seq_s, A_HEADS, A_HD))
```

```python
import functools

import jax
import jax.numpy as jnp
from jax import lax
from jax.experimental import pallas as pl
from jax.experimental.pallas import tpu as pltpu

F32 = jnp.float32
BF16 = jnp.bfloat16
I32 = jnp.int32
HIGHEST = lax.Precision.HIGHEST

DEPTH = 2
DN_ALPHA = (2.0 * DEPTH) ** 0.25
LN_EPS = 1e-5

M_HEADS = 8
M_DK = 64
M_DV = 128
MLSTM_CHUNK = 256
MLSTM_SAMPLE_CHUNK = 128

A_HEADS = 16
A_HD = 64
ROT_DIM = 16
ROPE_THETA = 500000.0
MOBA_BLOCK = 256
MOBA_TOPK = 3
PAGE_SIZE = 128

P_HEADS = 8
P_NKEYS = 128
P_TOPK = 16
P_SLOTS = P_HEADS * P_TOPK

ROW_TILE = 512
TOK_BLOCK = 128
NEG = -1e30
VMEM_LIMIT = 56 * 1024 * 1024

NT_DIMS = (((1,), (1,)), ((), ()))


def _cparams(semantics, vmem=VMEM_LIMIT):
    return pltpu.CompilerParams(dimension_semantics=semantics, vmem_limit_bytes=vmem)


def _mm_kernel(x_ref, w_ref, o_ref):
    o_ref[...] = jnp.dot(x_ref[...].astype(BF16), w_ref[...], preferred_element_type=F32)


def _matmul(x, w_bf16, tm=ROW_TILE):
    m, k = x.shape
    n = w_bf16.shape[1]
    return pl.pallas_call(
        _mm_kernel,
        grid=(m // tm,),
        in_specs=[pl.BlockSpec((tm, k), lambda i: (i, 0)),
                  pl.BlockSpec((k, n), lambda i: (0, 0))],
        out_specs=pl.BlockSpec((tm, n), lambda i: (i, 0)),
        out_shape=jax.ShapeDtypeStruct((m, n), F32),
        compiler_params=_cparams(("parallel",)),
        name="matmul",
    )(x, w_bf16)


def _layer_norm_rows(z, g, b):
    mu = jnp.mean(z, axis=-1, keepdims=True)
    zc = z - mu
    var = jnp.mean(zc * zc, axis=-1, keepdims=True)
    return zc * lax.rsqrt(var + LN_EPS) * g + b


def _mm_ln_kernel(a_ref, w_ref, x_ref, g_ref, b_ref, o_ref):
    y = jnp.dot(a_ref[...].astype(BF16), w_ref[...], preferred_element_type=F32)
    o_ref[...] = _layer_norm_rows(DN_ALPHA * x_ref[...] + y, g_ref[...], b_ref[...])


def _matmul_res_ln(a, w_bf16, x_res, g, b, tm=ROW_TILE):
    m, k = a.shape
    n = w_bf16.shape[1]
    return pl.pallas_call(
        _mm_ln_kernel,
        grid=(m // tm,),
        in_specs=[pl.BlockSpec((tm, k), lambda i: (i, 0)),
                  pl.BlockSpec((k, n), lambda i: (0, 0)),
                  pl.BlockSpec((tm, n), lambda i: (i, 0)),
                  pl.BlockSpec((1, n), lambda i: (0, 0)),
                  pl.BlockSpec((1, n), lambda i: (0, 0))],
        out_specs=pl.BlockSpec((tm, n), lambda i: (i, 0)),
        out_shape=jax.ShapeDtypeStruct((m, n), F32),
        compiler_params=_cparams(("parallel",)),
        name="matmul_res_ln",
    )(a, w_bf16, x_res, g.reshape(1, n), b.reshape(1, n))


def _res_ln_kernel(x_ref, y_ref, g_ref, b_ref, o_ref):
    o_ref[...] = _layer_norm_rows(DN_ALPHA * x_ref[...] + y_ref[...], g_ref[...], b_ref[...])


def _res_ln(x, y, g, b, tm=ROW_TILE):
    m, n = x.shape
    return pl.pallas_call(
        _res_ln_kernel,
        grid=(m // tm,),
        in_specs=[pl.BlockSpec((tm, n), lambda i: (i, 0)),
                  pl.BlockSpec((tm, n), lambda i: (i, 0)),
                  pl.BlockSpec((1, n), lambda i: (0, 0)),
                  pl.BlockSpec((1, n), lambda i: (0, 0))],
        out_specs=pl.BlockSpec((tm, n), lambda i: (i, 0)),
        out_shape=jax.ShapeDtypeStruct((m, n), F32),
        compiler_params=_cparams(("parallel",)),
        name="res_ln",
    )(x, y, g.reshape(1, n), b.reshape(1, n))


def _gates_kernel(x_ref, w_ref, b_ref, o_ref):
    g = jnp.dot(x_ref[...], w_ref[...], precision=HIGHEST, preferred_element_type=F32) + b_ref[...]
    lane = lax.broadcasted_iota(I32, g.shape, 1)
    log_sig = jnp.minimum(g, 0.0) - jnp.log1p(jnp.exp(-jnp.abs(g)))
    o_ref[...] = jnp.where(lane < M_HEADS, g, log_sig)


def _mlstm_gates(x, w_gate, b_gate, tm=ROW_TILE):
    m, k = x.shape
    w = jnp.zeros((k, 128), F32).at[:, :2 * M_HEADS].set(w_gate)
    b = jnp.zeros((1, 128), F32).at[0, :2 * M_HEADS].set(b_gate)
    return pl.pallas_call(
        _gates_kernel,
        grid=(m // tm,),
        in_specs=[pl.BlockSpec((tm, k), lambda i: (i, 0)),
                  pl.BlockSpec((k, 128), lambda i: (0, 0)),
                  pl.BlockSpec((1, 128), lambda i: (0, 0))],
        out_specs=pl.BlockSpec((tm, 128), lambda i: (i, 0)),
        out_shape=jax.ShapeDtypeStruct((m, 128), F32),
        compiler_params=_cparams(("parallel",)),
        name="mlstm_gates",
    )(x, w, b)


def _mlstm_kernel(q_ref, kt_ref, v_ref, o_ref, gc_ref, gr_ref, c0_ref, m0_ref, gain_ref,
                  hn_ref, cout_ref, mout_ref, c_sc, m_sc):
    c = pl.program_id(1)
    L = q_ref.shape[0]

    @pl.when(c == 0)
    def _():
        c_sc[...] = c0_ref[0]
        m_sc[...] = m0_ref[0]

    gc = gc_ref[...]
    gr = gr_ref[0]
    row = lax.broadcasted_iota(I32, (L, L), 0)
    col = lax.broadcasted_iota(I32, (L, L), 1)
    causal = col <= row
    b_col = jnp.dot(causal.astype(F32), gc, precision=HIGHEST, preferred_element_type=F32)
    b_row = jnp.dot(gr, (row <= col).astype(F32), precision=HIGHEST, preferred_element_type=F32)
    ones_lane0 = (lax.broadcasted_iota(I32, (L, 128), 1) == 0).astype(BF16)

    for h in range(M_HEADS):
        bc = b_col[:, M_HEADS + h:M_HEADS + h + 1]
        br = b_row[M_HEADS + h:M_HEADS + h + 1, :]
        igr = gr[h:h + 1, :]
        m_prev = m_sc[h:h + 1, 0:1]
        dlog = jnp.where(causal, bc - br + igr, NEG)
        m_inter = bc + m_prev
        m_t = jnp.maximum(m_inter, jnp.max(dlog, axis=1, keepdims=True))
        w_intra = jnp.exp(dlog - m_t)
        w_inter = jnp.exp(m_inter - m_t)

        qh = (q_ref[:, h * M_DK:(h + 1) * M_DK] * (M_DK ** -0.5)).astype(BF16)
        kth = kt_ref[0, h * M_DK:(h + 1) * M_DK, :]
        v_ext = jnp.concatenate(
            [v_ref[:, h * M_DV:(h + 1) * M_DV].astype(BF16), ones_lane0], axis=1)
        s = jnp.dot(qh, kth.astype(BF16), preferred_element_type=F32) * w_intra
        c_h = c_sc[h]
        nd = (jnp.dot(s.astype(BF16), v_ext, preferred_element_type=F32)
              + w_inter * jnp.dot(qh, c_h.astype(BF16), preferred_element_type=F32))
        num = nd[:, :M_DV]
        den = nd[:, M_DV:M_DV + 1]
        hh = num / jnp.maximum(jnp.abs(den), jnp.exp(-m_t))
        hh = hh * lax.rsqrt(jnp.mean(hh * hh, axis=1, keepdims=True) + 1e-6)
        og = o_ref[:, h * M_DV:(h + 1) * M_DV]
        hn_ref[:, h * M_DV:(h + 1) * M_DV] = (
            hh * gain_ref[:, h * M_DV:(h + 1) * M_DV] * (1.0 / (1.0 + jnp.exp(-og))))

        m_new = m_t[L - 1:L, :]
        b_last = bc[L - 1:L, :]
        decay = jnp.exp(b_last + m_prev - m_new)
        w_s = jnp.exp(b_last - br + igr - m_new)
        kw = (kth * w_s).astype(BF16)
        c_sc[h] = decay * c_h + jnp.dot(kw, v_ext, preferred_element_type=F32)
        m_sc[h:h + 1, :] = jnp.broadcast_to(m_new, (1, 128))

    @pl.when(c == pl.num_programs(1) - 1)
    def _():
        cout_ref[0] = c_sc[...]
        mout_ref[0] = m_sc[...]


def _mlstm(proj, kt, gates, gates_t, c0_ext, m0_b, gain, nb, seq, chunk):
    nc = seq // chunk
    qk = M_HEADS * M_DK
    vd = M_HEADS * M_DV
    row_blk = lambda b, c: (b * nc + c, 0)
    return pl.pallas_call(
        _mlstm_kernel,
        grid=(nb, nc),
        in_specs=[pl.BlockSpec((chunk, qk), row_blk),
                  pl.BlockSpec((1, qk, chunk), lambda b, c: (b, 0, c)),
                  pl.BlockSpec((chunk, vd), lambda b, c: (b * nc + c, 1)),
                  pl.BlockSpec((chunk, vd), lambda b, c: (b * nc + c, 2)),
                  pl.BlockSpec((chunk, 128), row_blk),
                  pl.BlockSpec((1, 2 * M_HEADS, chunk), lambda b, c: (b, 0, c)),
                  pl.BlockSpec((1, M_HEADS, M_DK, 2 * M_DV), lambda b, c: (b, 0, 0, 0)),
                  pl.BlockSpec((1, M_HEADS, 128), lambda b, c: (b, 0, 0)),
                  pl.BlockSpec((1, vd), lambda b, c: (0, 0))],
        out_specs=[pl.BlockSpec((chunk, vd), row_blk),
                   pl.BlockSpec((1, M_HEADS, M_DK, 2 * M_DV), lambda b, c: (b, 0, 0, 0)),
                   pl.BlockSpec((1, M_HEADS, 128), lambda b, c: (b, 0, 0))],
        out_shape=[jax.ShapeDtypeStruct((nb * seq, vd), F32),
                   jax.ShapeDtypeStruct((nb, M_HEADS, M_DK, 2 * M_DV), F32),
                   jax.ShapeDtypeStruct((nb, M_HEADS, 128), F32)],
        scratch_shapes=[pltpu.VMEM((M_HEADS, M_DK, 2 * M_DV), F32),
                        pltpu.VMEM((M_HEADS, 128), F32)],
        compiler_params=_cparams(("parallel", "arbitrary")),
        name="mlstm",
    )(proj, kt, proj, proj, gates, gates_t, c0_ext, m0_b, gain.reshape(1, vd))


def _mlstm_layer(x_all, n_p, nb_p, seq_p, nb_s, seq_s, st_c, st_n, st_m, w_in, b_gate, gain):
    nt = x_all.shape[0]
    qk = M_HEADS * M_DK
    vd = M_HEADS * M_DV
    n_s = nb_s * seq_s
    proj = _matmul(x_all, w_in[:, :2 * qk + 2 * vd].astype(BF16))
    gates = _mlstm_gates(x_all, w_in[:, 2 * qk + 2 * vd:], b_gate)

    kt_p = proj[:n_p, qk:2 * qk].reshape(nb_p, seq_p, qk).transpose(0, 2, 1)
    gt_p = gates[:n_p, :2 * M_HEADS].reshape(nb_p, seq_p, 2 * M_HEADS).transpose(0, 2, 1)
    c0_p = jnp.zeros((nb_p, M_HEADS, M_DK, 2 * M_DV), F32)
    m0_p = jnp.zeros((nb_p, M_HEADS, 128), F32)
    hn_p, c_p, m_p = _mlstm(proj, kt_p, gates, gt_p, c0_p, m0_p, gain, nb_p, seq_p,
                            min(MLSTM_CHUNK, seq_p))

    ls = MLSTM_SAMPLE_CHUNK
    pad = ((0, 0), (0, ls - seq_s), (0, 0))
    proj_s = jnp.pad(proj[n_p:n_p + n_s].reshape(nb_s, seq_s, -1), pad)
    g_s = gates[n_p:n_p + n_s].reshape(nb_s, seq_s, 128)
    pad_row = jnp.where(jnp.arange(128) < M_HEADS, NEG, 0.0).astype(F32)
    g_s = jnp.concatenate([g_s, jnp.broadcast_to(pad_row, (nb_s, ls - seq_s, 128))], axis=1)
    kt_s = proj_s[:, :, qk:2 * qk].transpose(0, 2, 1)
    gt_s = g_s[:, :, :2 * M_HEADS].transpose(0, 2, 1)
    c0_s = jnp.concatenate([st_c, st_n[..., None],
                            jnp.zeros(st_c.shape[:3] + (M_DV - 1,), F32)], axis=-1)
    m0_s = jnp.broadcast_to(st_m[..., None], st_m.shape + (128,))
    hn_s, c_s, m_s = _mlstm(proj_s.reshape(nb_s * ls, -1), kt_s, g_s.reshape(nb_s * ls, 128), gt_s,
                            c0_s, m0_s, gain, nb_s, ls, ls)
    hn_s = hn_s.reshape(nb_s, ls, vd)[:, :seq_s].reshape(n_s, vd)
    hn = jnp.concatenate([hn_p, hn_s, jnp.zeros((nt - n_p - n_s, vd), F32)], axis=0)
    states = (c_p[..., :M_DV], c_p[..., M_DV], m_p[..., 0],
              c_s[..., :M_DV], c_s[..., M_DV], m_s[..., 0])
    return hn, states


def _rope_kernel(q_ref, k_ref, ca_ref, cb_ref, cc_ref, qo_ref, ko_ref, km_ref):
    ca, cb, cc = ca_ref[...], cb_ref[...], cc_ref[...]
    half = ROT_DIM // 2
    for j in range(q_ref.shape[1] // 128):
        sl = slice(j * 128, (j + 1) * 128)
        for src, dst in ((q_ref, qo_ref), (k_ref, ko_ref)):
            x = src[:, sl]
            dst[:, sl] = (x * ca + pltpu.roll(x, 128 - half, 1) * cb + pltpu.roll(x, half, 1) * cc)
    km_ref[0] = jnp.mean(ko_ref[...], axis=0, keepdims=True)


def _rope_tables(pos):
    half = ROT_DIM // 2
    inv = ROPE_THETA ** (-jnp.arange(0, ROT_DIM, 2, dtype=F32) / ROT_DIM)
    ang = pos.astype(F32)[:, None] * inv[None, :]
    cos, sin = jnp.cos(ang), jnp.sin(ang)
    lane = jnp.arange(128) % A_HD
    f = lane % half
    in_lo = lane < half
    in_hi = (lane >= half) & (lane < ROT_DIM)
    ca = jnp.where((in_lo | in_hi)[None, :], cos[:, f], 1.0)
    cb = jnp.where(in_lo[None, :], -sin[:, f], 0.0)
    cc = jnp.where(in_hi[None, :], sin[:, f], 0.0)
    return ca.astype(F32), cb.astype(F32), cc.astype(F32)


def _rope(qkv, tables, d):
    nt = qkv.shape[0]
    tm = MOBA_BLOCK
    spec_t = pl.BlockSpec((tm, 128), lambda i: (i, 0))
    return pl.pallas_call(
        _rope_kernel,
        grid=(nt // tm,),
        in_specs=[pl.BlockSpec((tm, d), lambda i: (i, 0)),
                  pl.BlockSpec((tm, d), lambda i: (i, 1)),
                  spec_t, spec_t, spec_t],
        out_specs=[pl.BlockSpec((tm, d), lambda i: (i, 0)),
                   pl.BlockSpec((tm, d), lambda i: (i, 0)),
                   pl.BlockSpec((1, 1, d), lambda i: (i, 0, 0))],
        out_shape=[jax.ShapeDtypeStruct((nt, d), F32),
                   jax.ShapeDtypeStruct((nt, d), F32),
                   jax.ShapeDtypeStruct((nt // tm, 1, d), F32)],
        compiler_params=_cparams(("parallel",)),
        name="rope",
    )(qkv, qkv, *tables)


def _moba_prompt_kernel(q_ref, k_ref, vt_ref, km_ref, o_ref):
    qi = pl.program_id(2)
    blk = MOBA_BLOCK
    nblk = km_ref.shape[0]
    q = q_ref[...] * (A_HD ** -0.5)
    lane = lax.broadcasted_iota(I32, (blk, 128), 1)
    key_i = lax.broadcasted_iota(I32, (blk, blk), 0)
    qry_i = lax.broadcasted_iota(I32, (blk, blk), 1)
    jrow = lax.broadcasted_iota(I32, (nblk, blk), 0)
    start = pl.multiple_of(qi * blk, blk)
    k_own = k_ref[pl.ds(start, blk), :].astype(BF16)
    vt_own = vt_ref[qi].astype(BF16)

    qbs, sels, init = [], [], []
    for hh in range(2):
        qh = jnp.where((lane // A_HD) == hh, q, 0.0)
        qb = qh.astype(BF16)
        gate = lax.dot_general(km_ref[...], qh, NT_DIMS, precision=HIGHEST,
                               preferred_element_type=F32)
        valid = jrow < qi
        gm = jnp.where(valid, gate, NEG)
        rank = jnp.zeros((nblk, blk), I32)
        for jp in range(nblk - 1):
            gj = gm[jp:jp + 1, :]
            beats = (gj > gm) | ((gj == gm) & (jp < jrow))
            rank = rank + jnp.where(beats, 1, 0)
        sels.append(jnp.where(valid & (rank < MOBA_TOPK), 1.0, 0.0))
        qbs.append(qb)

        s = lax.dot_general(k_own, qb, NT_DIMS, preferred_element_type=F32)
        s = jnp.where(key_i <= qry_i, s, NEG)
        m0 = jnp.max(s, axis=0, keepdims=True)
        p = jnp.exp(s - m0)
        init += [m0, jnp.sum(p, axis=0, keepdims=True),
                 jnp.dot(vt_own, p.astype(BF16), preferred_element_type=F32)]

    def scores(j):
        kj = k_ref[pl.ds(pl.multiple_of(j * blk, blk), blk), :].astype(BF16)
        return [lax.dot_general(kj, qbs[hh], NT_DIMS, preferred_element_type=F32) for hh in range(2)]

    def body(j, carry):
        s_next = scores(jnp.minimum(j + 1, qi))
        vtj = vt_ref[j].astype(BF16)
        new = []
        for hh in range(2):
            m, l, acc = carry[3 * hh:3 * hh + 3]
            selj = jnp.sum(jnp.where(jrow == j, sels[hh], 0.0), axis=0, keepdims=True)
            sj = jnp.where(selj > 0.0, carry[6 + hh], NEG)
            m_new = jnp.maximum(m, jnp.max(sj, axis=0, keepdims=True))
            alpha = jnp.exp(m - m_new)
            pj = jnp.exp(sj - m_new)
            new += [m_new, alpha * l + jnp.sum(pj, axis=0, keepdims=True),
                    alpha * acc + jnp.dot(vtj, pj.astype(BF16), preferred_element_type=F32)]
        return tuple(new + s_next)

    fin = lax.fori_loop(0, qi, body, tuple(init + scores(0)))
    drow = lax.broadcasted_iota(I32, (128, blk), 0)
    out_t = jnp.where(drow < A_HD, fin[2] / fin[1], fin[5] / fin[4])
    o_ref[...] = out_t.T


def _moba_prompt(q_rot, k_rot, qkv, kmean, nb, seq, d):
    nq = seq // MOBA_BLOCK
    ng = d // 128
    v_t = qkv[:nb * seq, 2 * d:].reshape(nb * nq, MOBA_BLOCK, d).transpose(0, 2, 1)
    return pl.pallas_call(
        _moba_prompt_kernel,
        grid=(nb, ng, nq),
        in_specs=[pl.BlockSpec((MOBA_BLOCK, 128), lambda b, g, i: (b * nq + i, g)),
                  pl.BlockSpec((seq, 128), lambda b, g, i: (b, g)),
                  pl.BlockSpec((nq, 128, MOBA_BLOCK), lambda b, g, i: (b, g, 0)),
                  pl.BlockSpec((nq, 128), lambda b, g, i: (b, g))],
        out_specs=pl.BlockSpec((MOBA_BLOCK, 128), lambda b, g, i: (b * nq + i, g)),
        out_shape=jax.ShapeDtypeStruct((nb * seq, d), F32),
        compiler_params=_cparams(("parallel", "parallel", "arbitrary")),
        name="moba_prompt",
    )(q_rot, k_rot, v_t, kmean)


HEAD_BATCH = (((2,), (1,)), ((0,), (0,)))
HEAD_BATCH_NT = (((2,), (2,)), ((0,), (0,)))
T_PAD = 8


def _moba_sample_kernel(pt_ref, q_ref, kn_ref, vn_ref, k0_ref, k1_ref, v0_ref, v1_ref, o_ref,
                        m_sc, l_sc, g_sc, acc_sc):
    j = pl.program_id(1)
    n_blk = acc_sc.shape[0]
    stat_shape = m_sc.shape

    @pl.when(j == 0)
    def _():
        m_sc[...] = jnp.full(stat_shape, NEG, F32)
        l_sc[...] = jnp.zeros(stat_shape, F32)
        g_sc[...] = jnp.zeros(stat_shape, F32)

    qb = (q_ref[0] * (A_HD ** -0.5)).astype(BF16)
    s = jnp.concatenate(
        [lax.dot_general(qb, kr[0].astype(BF16), HEAD_BATCH, preferred_element_type=F32)
         for kr in (k0_ref, k1_ref)], axis=-1)
    m = jnp.max(s, axis=-1, keepdims=True)
    pw = jnp.exp(s - m)
    half = PAGE_SIZE
    acc_sc[j] = (lax.dot_general(pw[..., :half].astype(BF16), v0_ref[0].astype(BF16), HEAD_BATCH_NT,
                                 preferred_element_type=F32)
                 + lax.dot_general(pw[..., half:].astype(BF16), v1_ref[0].astype(BF16), HEAD_BATCH_NT,
                                   preferred_element_type=F32))
    col = lax.broadcasted_iota(I32, stat_shape, 2) == j
    m_sc[...] = jnp.where(col, m, m_sc[...])
    l_sc[...] = jnp.where(col, jnp.sum(pw, axis=-1, keepdims=True), l_sc[...])
    g_sc[...] = jnp.where(col, jnp.sum(s, axis=-1, keepdims=True), g_sc[...])

    @pl.when(j == n_blk - 1)
    def _():
        jidx = lax.broadcasted_iota(I32, stat_shape, 2)
        gate = jnp.where(jidx < n_blk, g_sc[...], -jnp.inf)
        sel = jnp.zeros(stat_shape, F32)
        for _ in range(min(MOBA_TOPK, n_blk)):
            gmax = jnp.max(gate, axis=-1, keepdims=True)
            first = jnp.min(jnp.where(gate == gmax, jidx, n_blk), axis=-1, keepdims=True)
            hit = jidx == first
            sel = jnp.where(hit, 1.0, sel)
            gate = jnp.where(hit, -jnp.inf, gate)
        so = lax.dot_general(qb, kn_ref[0].astype(BF16), HEAD_BATCH_NT, preferred_element_type=F32)
        t_q = lax.broadcasted_iota(I32, so.shape, 1)
        t_k = lax.broadcasted_iota(I32, so.shape, 2)
        so = jnp.where(t_k <= t_q, so, NEG)
        m_blk = jnp.where(sel > 0.0, m_sc[...], NEG)
        m_tot = jnp.maximum(jnp.max(so, axis=-1, keepdims=True), jnp.max(m_blk, axis=-1, keepdims=True))
        po = jnp.exp(so - m_tot)
        wgt = jnp.where(sel > 0.0, jnp.exp(m_blk - m_tot), 0.0)
        l_tot = jnp.sum(po, axis=-1, keepdims=True) + jnp.sum(wgt * l_sc[...], axis=-1, keepdims=True)
        acc = lax.dot_general(po.astype(BF16), vn_ref[0].astype(BF16), HEAD_BATCH, preferred_element_type=F32)
        for jb in range(n_blk):
            acc = acc + wgt[:, :, jb:jb + 1] * acc_sc[jb]
        o_ref[0] = acc / l_tot


def _moba_sample(q_h, k_h, v_h, cache_kt, cache_vt, page_table_flat):
    nb, nh, t_pad, hd = q_h.shape
    ppb = MOBA_BLOCK // PAGE_SIZE
    assert ppb == 2
    n_pages = page_table_flat.shape[0] // nb
    n_blk = n_pages // ppb
    lanes = -(-n_blk // 128) * 128
    seq_map = lambda b, j, pt: (b, 0, 0, 0)
    page_spec = lambda r: pl.BlockSpec((1, nh, hd, PAGE_SIZE),
                                       lambda b, j, pt: (pt[b * n_pages + j * ppb + r], 0, 0, 0))
    grid_spec = pltpu.PrefetchScalarGridSpec(
        num_scalar_prefetch=1,
        grid=(nb, n_blk),
        in_specs=[pl.BlockSpec((1, nh, t_pad, hd), seq_map),
                  pl.BlockSpec((1, nh, t_pad, hd), seq_map),
                  pl.BlockSpec((1, nh, t_pad, hd), seq_map),
                  page_spec(0), page_spec(1), page_spec(0), page_spec(1)],
        out_specs=pl.BlockSpec((1, nh, t_pad, hd), seq_map),
        scratch_shapes=[pltpu.VMEM((nh, t_pad, lanes), F32),
                        pltpu.VMEM((nh, t_pad, lanes), F32),
                        pltpu.VMEM((nh, t_pad, lanes), F32),
                        pltpu.VMEM((n_blk, nh, t_pad, hd), F32)],
    )
    return pl.pallas_call(
        _moba_sample_kernel,
        grid_spec=grid_spec,
        out_shape=jax.ShapeDtypeStruct((nb, nh, t_pad, hd), F32),
        compiler_params=_cparams(("parallel", "arbitrary")),
        name="moba_sample",
    )(page_table_flat, q_h, k_h, v_h, cache_kt, cache_kt, cache_vt, cache_vt)


def _top16_desc(vals, payload, n_rows):
    t = vals.shape[1]
    r16 = lax.broadcasted_iota(I32, (P_TOPK, t), 0)
    out_v = jnp.zeros((P_TOPK, t), F32)
    out_p = jnp.zeros((P_TOPK, t), I32)
    big = jnp.int32(2 ** 30)
    for r in range(P_TOPK):
        m = jnp.max(vals, axis=0, keepdims=True)
        pick = jnp.min(jnp.where(vals == m, payload, big), axis=0, keepdims=True)
        vals = jnp.where(payload == pick, -jnp.inf, vals)
        out_v = jnp.where(r16 == r, m, out_v)
        out_p = jnp.where(r16 == r, pick, out_p)
    return out_v, out_p


def _peer_topk_kernel(q_ref, keys_ref, pidx_ref, g_ref):
    t = q_ref.shape[0]
    kio = lax.broadcasted_iota(I32, (P_NKEYS, t), 0)
    e_heads, g_heads = [], []
    for h in range(P_HEADS):
        sv, si = [], []
        for p in range(2):
            c0 = (h * 2 + p) * (P_NKEYS)
            scores = lax.dot_general(keys_ref[h, p], q_ref[:, c0:c0 + 128], NT_DIMS,
                                     precision=HIGHEST, preferred_element_type=F32)
            v, i = _top16_desc(scores, kio, P_NKEYS)
            sv.append(v)
            si.append(i)
        half = P_TOPK // 2
        cand = jnp.concatenate(
            [sv[0][0:1, :] + sv[1]]
            + [sv[0][i:i + 1, :] + sv[1][0:half, :] for i in range(1, half)]
            + [sv[0][half:, :] + sv[1][0:1, :]], axis=0)
        eid = jnp.concatenate(
            [si[0][0:1, :] * P_NKEYS + si[1]]
            + [si[0][i:i + 1, :] * P_NKEYS + si[1][0:half, :] for i in range(1, half)]
            + [si[0][half:, :] * P_NKEYS + si[1][0:1, :]], axis=0)
        cv, ce = _top16_desc(cand, eid, cand.shape[0])
        ex = jnp.exp(cv - cv[0:1, :])
        g_heads.append(ex / jnp.sum(ex, axis=0, keepdims=True))
        e_heads.append(ce)
    e_all = jnp.concatenate(e_heads, axis=0).T
    pidx_ref[...] = e_all
    g_ref[...] = jnp.concatenate(g_heads, axis=0).T


def _peer_topk(q, keys):
    nt = q.shape[0]
    t = TOK_BLOCK
    return pl.pallas_call(
        _peer_topk_kernel,
        grid=(nt // t,),
        in_specs=[pl.BlockSpec((t, q.shape[1]), lambda i: (i, 0)),
                  pl.BlockSpec(keys.shape, lambda i: (0, 0, 0, 0))],
        out_specs=[pl.BlockSpec((t, P_SLOTS), lambda i: (i, 0)),
                   pl.BlockSpec((t, P_SLOTS), lambda i: (i, 0))],
        out_shape=[jax.ShapeDtypeStruct((nt, P_SLOTS), I32),
                   jax.ShapeDtypeStruct((nt, P_SLOTS), F32)],
        compiler_params=_cparams(("parallel",)),
        name="peer_topk",
    )(q, keys)


GATHER_CHUNK = 32
TILE_ROWS = 8
SLOT_COLS = P_SLOTS * TILE_ROWS


def _gather_tiles(tab_ref, idx_smem, slot, j, c):
    return jnp.concatenate(
        [tab_ref[idx_smem[slot, j, c * GATHER_CHUNK + i]] for i in range(GATHER_CHUNK)], axis=0)


def _index_copy(pidx_ref, idx_smem, sem, g8, slot):
    return pltpu.make_async_copy(pidx_ref.at[pl.ds(pl.multiple_of(g8 * 8, 8), 8), :],
                                 idx_smem.at[slot], sem.at[slot])


def _for_token_groups(pidx_ref, idx_smem, sem, n_groups, body):
    _index_copy(pidx_ref, idx_smem, sem, 0, 0).start()

    def pair(gp, carry):
        for slot in range(2):
            g8 = gp * 2 + slot
            _index_copy(pidx_ref, idx_smem, sem, g8, slot).wait()

            @pl.when(g8 + 1 < n_groups)
            def _():
                _index_copy(pidx_ref, idx_smem, sem, g8 + 1, 1 - slot).start()

            body(g8, slot)
        return carry

    lax.fori_loop(0, n_groups // 2, pair, 0)


def _split_bf16(x):
    hi = x.astype(BF16)
    lo = (x - hi.astype(F32)).astype(BF16)
    return jnp.concatenate([hi, lo], axis=0)


def _fold_row_mask(width):
    r = lax.broadcasted_iota(I32, (8, width), 1) % TILE_ROWS
    c = lax.broadcasted_iota(I32, (8, width), 0)
    return r == c


def _peer_u_kernel(pidx_ref, xf_ref, g_ref, tab_ref, exp_ref, w_ref, hid_sc, idx_smem, sem):
    tb = g_ref.shape[0]
    n_chunk = P_SLOTS // GATHER_CHUNK
    cw = GATHER_CHUNK * TILE_ROWS
    keep = _fold_row_mask(cw)

    def group(g8, slot):
        base = pl.multiple_of(g8 * 8, 8)
        rows = []
        for j in range(8):
            n = base + j
            x16 = _split_bf16(xf_ref[pl.ds(pl.multiple_of(n * 8, 8), 8), :])
            parts = []
            for c in range(n_chunk):
                y = lax.dot_general(x16, _gather_tiles(tab_ref, idx_smem, slot, j, c), NT_DIMS,
                                    preferred_element_type=F32)
                parts.append(jnp.sum(jnp.where(keep, y[0:8] + y[8:16], 0.0), axis=0, keepdims=True))
            rows.append(jnp.concatenate(parts, axis=1))
        s16 = _split_bf16(jnp.concatenate(rows, axis=0))
        h = lax.dot_general(s16, exp_ref[...], NT_DIMS, preferred_element_type=F32)
        hid_sc[pl.ds(base, 8), :] = h[0:8] + h[8:16]

    _for_token_groups(pidx_ref, idx_smem, sem, tb // 8, group)
    hid = hid_sc[...]
    gelu = 0.5 * hid * (1.0 + lax.erf(hid * (2.0 ** -0.5)))
    w_ref[...] = g_ref[...] * gelu


def _peer_v_kernel(pidx_ref, w_ref, tab_ref, exp_ref, o_ref, wx_sc, idx_smem, sem):
    tb = w_ref.shape[0]
    n_chunk = P_SLOTS // GATHER_CHUNK
    cw = GATHER_CHUNK * TILE_ROWS
    keep = _fold_row_mask(SLOT_COLS)
    w = w_ref[...]
    w_hi = w.astype(BF16)
    w_lo = (w - w_hi.astype(F32)).astype(BF16)
    wx_sc[...] = (jnp.dot(w_hi, exp_ref[...], preferred_element_type=F32)
                  + jnp.dot(w_lo, exp_ref[...], preferred_element_type=F32))

    def group(g8, slot):
        base = pl.multiple_of(g8 * 8, 8)
        w_blk = wx_sc[pl.ds(base, 8), :]
        for j in range(8):
            n = base + j
            lhs =_split_bf16(jnp.where(keep, w_blk[j:j + 1, :], 0.0))
            acc = jnp.zeros((16, 128), F32)
            for c in range(n_chunk):
                acc = acc + jnp.dot(lhs[:, c * cw:(c + 1) * cw], _gather_tiles(tab_ref, idx_smem, slot, j, c),
                                    preferred_element_type=F32)
            o_ref[pl.ds(pl.multiple_of(n * 8, 8), 8), :] = acc[0:8] + acc[8:16]

    _for_token_groups(pidx_ref, idx_smem, sem, tb // 8, group)


def _peer_gather(x, pidx, g, u_rows, v_rows):
    nt, d = x.shape
    fold = d // 128
    tb = TOK_BLOCK
    tab_spec = pl.BlockSpec(u_rows.shape, lambda i: (0, 0, 0), pipeline_mode=pl.Buffered(1))
    slot_spec = pl.BlockSpec((tb, P_SLOTS), lambda i: (i, 0))
    idx_staging = [pltpu.SMEM((2, 8, P_SLOTS), I32), pltpu.SemaphoreType.DMA((2,))]
    exp_spec = pl.BlockSpec((P_SLOTS, SLOT_COLS), lambda i: (0, 0))
    expand = (jnp.arange(SLOT_COLS)[None, :] // TILE_ROWS == jnp.arange(P_SLOTS)[:, None]).astype(BF16)
    w = pl.pallas_call(
        _peer_u_kernel,
        grid=(nt // tb,),
        in_specs=[slot_spec, pl.BlockSpec((tb * fold, 128), lambda i: (i, 0)), slot_spec, tab_spec, exp_spec],
        out_specs=slot_spec,
        out_shape=jax.ShapeDtypeStruct((nt, P_SLOTS), F32),
        scratch_shapes=[pltpu.VMEM((tb, P_SLOTS), F32)] + idx_staging,
        compiler_params=_cparams(("parallel",)),
        name="peer_u",
    )(pidx, x.reshape(nt * fold, 128), g, u_rows, expand)
    out = pl.pallas_call(
        _peer_v_kernel,
        grid=(nt // tb,),
        in_specs=[slot_spec, slot_spec, tab_spec, exp_spec],
        out_specs=pl.BlockSpec((tb * fold, 128), lambda i: (i, 0)),
        out_shape=jax.ShapeDtypeStruct((nt * fold, 128), F32),
        scratch_shapes=[pltpu.VMEM((tb, SLOT_COLS), F32)] + idx_staging,
        compiler_params=_cparams(("parallel",)),
        name="peer_v",
    )(pidx, w, v_rows, expand)
    return out.reshape(nt, d)


def _fold_table(tab):
    e, d = tab.shape
    return tab.astype(BF16).reshape(e, d // 128, 128)


def _peer_layer(x, w_q, keys, u_tab, v_tab, ln_g, ln_b):
    q = _matmul(x, w_q.astype(BF16))
    pidx, g = _peer_topk(q, keys)
    y = _peer_gather(x, pidx, g, _fold_table(u_tab), _fold_table(v_tab))
    return _res_ln(x, y, ln_g, ln_b)


def kernel(x_prompt, x_sample, state_mlstm_C, state_mlstm_n, state_mlstm_m, cache_k, cache_v, page_table,
           w_in_a, b_gate_a, mh_norm_a, w_out_a, w_kv, w_q_b, w_out_b,
           ln_mix_g, ln_mix_b, ln_ffn_g, ln_ffn_b, peer_wq, peer_keys, peer_u, peer_v):
    nb_p, seq_p, d = x_prompt.shape
    nb_s, seq_s, _ = x_sample.shape
    n_p, n_s = nb_p * seq_p, nb_s * seq_s
    nt = -(-(n_p + n_s) // ROW_TILE) * ROW_TILE
    past_len = page_table.shape[1] * PAGE_SIZE
    ppb = MOBA_BLOCK // PAGE_SIZE
    assert past_len % MOBA_BLOCK == 0 and seq_p % MOBA_BLOCK == 0 and n_p % ROW_TILE == 0
    assert state_mlstm_C.shape[0] == 1 and w_q_b.shape[0] == 1

    x = jnp.concatenate([x_prompt.reshape(n_p, d), x_sample.reshape(n_s, d),
                         jnp.zeros((nt - n_p - n_s, d), F32)], axis=0)

    hn, states = _mlstm_layer(x, n_p, nb_p, seq_p, nb_s, seq_s,
                              state_mlstm_C[0], state_mlstm_n[0], state_mlstm_m[0],
                              w_in_a[0], b_gate_a[0], mh_norm_a[0])
    x = _matmul_res_ln(hn, w_out_a[0].astype(BF16), x, ln_mix_g[0], ln_mix_b[0])
    x = _peer_layer(x, peer_wq[0], peer_keys[0], peer_u[0], peer_v[0], ln_ffn_g[0], ln_ffn_b[0])

    qkv = _matmul(x, jnp.concatenate([w_q_b[0], w_kv], axis=1).astype(BF16))
    pos = jnp.concatenate([jnp.tile(jnp.arange(seq_p), nb_p),
                           past_len + jnp.tile(jnp.arange(seq_s), nb_s),
                           jnp.zeros((nt - n_p - n_s,), I32)])
    q_rot, k_rot, kmean = _rope(qkv, _rope_tables(pos), d)
    attn_p = _moba_prompt(q_rot, k_rot, qkv, kmean.reshape(nt // MOBA_BLOCK, d), nb_p, seq_p, d)

    q_s = q_rot[n_p:n_p + n_s].reshape(nb_s, seq_s, d)
    k_s = k_rot[n_p:n_p + n_s].reshape(nb_s, seq_s, d)
    v_s = qkv[n_p:n_p + n_s, 2 * d:].reshape(nb_s, seq_s, d)
    pt_flat = page_table.reshape(-1).astype(I32)

    def head_major(a):
        a = a.reshape(nb_s, seq_s, A_HEADS, A_HD).transpose(0, 2, 1, 3)
        return jnp.pad(a, ((0, 0), (0, 0), (0, T_PAD - seq_s), (0, 0)))

    attn_s = _moba_sample(head_major(q_s), head_major(k_s), head_major(v_s),
                          cache_k.transpose(0, 2, 3, 1), cache_v.transpose(0, 2, 3, 1), pt_flat)
    attn_s = attn_s[:, :, :seq_s, :].transpose(0, 2, 1, 3)

    attn = jnp.concatenate([attn_p, attn_s.reshape(n_s, d), jnp.zeros((nt - n_p - n_s, d), F32)], axis=0)
    x = _matmul_res_ln(attn, w_out_b[0].astype(BF16), x, ln_mix_g[1], ln_mix_b[1])
    x = _peer_layer(x, peer_wq[1], peer_keys[1], peer_u[1], peer_v[1], ln_ffn_g[1], ln_ffn_b[1])

    y_prompt = x[:n_p].reshape(nb_p, seq_p, d)
    y_sample = x[n_p:n_p + n_s].reshape(nb_s, seq_s, d)
    c_p, n_pr, m_p, c_s, n_sm, m_s = states
    k_prompt = k_rot[:n_p].reshape(nb_p, seq_p, A_HEADS, A_HD)
    v_prompt = qkv[:n_p, 2 * d:].reshape(nb_p, seq_p, A_HEADS, A_HD)
    return (y_prompt, y_sample, c_p[None], n_pr[None], m_p[None], c_s[None], n_sm[None], m_s[None],
            k_prompt, v_prompt, k_s.reshape(nb_s, seq_s, A_HEADS, A_HD), v_s.reshape(nb_s, seq_s, A_HEADS, A_HD))
```

```python
import functools

import jax
import jax.numpy as jnp
from jax import lax
from jax.experimental import pallas as pl
from jax.experimental.pallas import tpu as pltpu

F32 = jnp.float32
BF16 = jnp.bfloat16
I32 = jnp.int32
HIGHEST = lax.Precision.HIGHEST

DEPTH = 2
DN_ALPHA = (2.0 * DEPTH) ** 0.25
LN_EPS = 1e-5

M_HEADS = 8
M_DK = 64
M_DV = 128
MLSTM_CHUNK = 256
MLSTM_SAMPLE_CHUNK = 128

A_HEADS = 16
A_HD = 64
ROT_DIM = 16
ROPE_THETA = 500000.0
MOBA_BLOCK = 256
MOBA_TOPK = 3
PAGE_SIZE = 128

P_HEADS = 8
P_NKEYS = 128
P_TOPK = 16
P_SLOTS = P_HEADS * P_TOPK

ROW_TILE = 512
TOK_BLOCK = 128
NEG = -1e30
VMEM_LIMIT = 56 * 1024 * 1024

NT_DIMS = (((1,), (1,)), ((), ()))


def _cparams(semantics, vmem=VMEM_LIMIT):
    return pltpu.CompilerParams(dimension_semantics=semantics, vmem_limit_bytes=vmem)


def _mm_kernel(x_ref, w_ref, o_ref):
    o_ref[...] = jnp.dot(x_ref[...].astype(BF16), w_ref[...], preferred_element_type=F32)


def _matmul(x, w_bf16, tm=ROW_TILE):
    m, k = x.shape
    n = w_bf16.shape[1]
    return pl.pallas_call(
        _mm_kernel,
        grid=(m // tm,),
        in_specs=[pl.BlockSpec((tm, k), lambda i: (i, 0)),
                  pl.BlockSpec((k, n), lambda i: (0, 0))],
        out_specs=pl.BlockSpec((tm, n), lambda i: (i, 0)),
        out_shape=jax.ShapeDtypeStruct((m, n), F32),
        compiler_params=_cparams(("parallel",)),
        name="matmul",
    )(x, w_bf16)


def _layer_norm_rows(z, g, b):
    mu = jnp.mean(z, axis=-1, keepdims=True)
    zc = z - mu
    var = jnp.mean(zc * zc, axis=-1, keepdims=True)
    return zc * lax.rsqrt(var + LN_EPS) * g + b


def _mm_ln_kernel(a_ref, w_ref, x_ref, g_ref, b_ref, o_ref):
    y = jnp.dot(a_ref[...].astype(BF16), w_ref[...], preferred_element_type=F32)
    o_ref[...] = _layer_norm_rows(DN_ALPHA * x_ref[...] + y, g_ref[...], b_ref[...])


def _matmul_res_ln(a, w_bf16, x_res, g, b, tm=ROW_TILE):
    m, k = a.shape
    n = w_bf16.shape[1]
    return pl.pallas_call(
        _mm_ln_kernel,
        grid=(m // tm,),
        in_specs=[pl.BlockSpec((tm, k), lambda i: (i, 0)),
                  pl.BlockSpec((k, n), lambda i: (0, 0)),
                  pl.BlockSpec((tm, n), lambda i: (i, 0)),
                  pl.BlockSpec((1, n), lambda i: (0, 0)),
                  pl.BlockSpec((1, n), lambda i: (0, 0))],
        out_specs=pl.BlockSpec((tm, n), lambda i: (i, 0)),
        out_shape=jax.ShapeDtypeStruct((m, n), F32),
        compiler_params=_cparams(("parallel",)),
        name="matmul_res_ln",
    )(a, w_bf16, x_res, g.reshape(1, n), b.reshape(1, n))


def _res_ln_kernel(x_ref, y_ref, g_ref, b_ref, o_ref):
    o_ref[...] = _layer_norm_rows(DN_ALPHA * x_ref[...] + y_ref[...], g_ref[...], b_ref[...])


def _res_ln(x, y, g, b, tm=ROW_TILE):
    m, n = x.shape
    return pl.pallas_call(
        _res_ln_kernel,
        grid=(m // tm,),
        in_specs=[pl.BlockSpec((tm, n), lambda i: (i, 0)),
                  pl.BlockSpec((tm, n), lambda i: (i, 0)),
                  pl.BlockSpec((1, n), lambda i: (0, 0)),
                  pl.BlockSpec((1, n), lambda i: (0, 0))],
        out_specs=pl.BlockSpec((tm, n), lambda i: (i, 0)),
        out_shape=jax.ShapeDtypeStruct((m, n), F32),
        compiler_params=_cparams(("parallel",)),
        name="res_ln",
    )(x, y, g.reshape(1, n), b.reshape(1, n))


def _gates_kernel(x_ref, w_ref, b_ref, o_ref):
    g = jnp.dot(x_ref[...], w_ref[...], precision=HIGHEST, preferred_element_type=F32) + b_ref[...]
    lane = lax.broadcasted_iota(I32, g.shape, 1)
    log_sig = jnp.minimum(g, 0.0) - jnp.log1p(jnp.exp(-jnp.abs(g)))
    o_ref[...] = jnp.where(lane < M_HEADS, g, log_sig)


def _mlstm_gates(x, w_gate, b_gate, tm=ROW_TILE):
    m, k = x.shape
    w = jnp.zeros((k, 128), F32).at[:, :2 * M_HEADS].set(w_gate)
    b = jnp.zeros((1, 128), F32).at[0, :2 * M_HEADS].set(b_gate)
    return pl.pallas_call(
        _gates_kernel,
        grid=(m // tm,),
        in_specs=[pl.BlockSpec((tm, k), lambda i: (i, 0)),
                  pl.BlockSpec((k, 128), lambda i: (0, 0)),
                  pl.BlockSpec((1, 128), lambda i: (0, 0))],
        out_specs=pl.BlockSpec((tm, 128), lambda i: (i, 0)),
        out_shape=jax.ShapeDtypeStruct((m, 128), F32),
        compiler_params=_cparams(("parallel",)),
        name="mlstm_gates",
    )(x, w, b)


def _mlstm_kernel(q_ref, kt_ref, v_ref, o_ref, gc_ref, gr_ref, c0_ref, m0_ref, gain_ref,
                  hn_ref, cout_ref, mout_ref, c_sc, m_sc):
    c = pl.program_id(1)
    L = q_ref.shape[0]

    @pl.when(c == 0)
    def _():
        c_sc[...] = c0_ref[0]
        m_sc[...] = m0_ref[0]

    gc = gc_ref[...]
    gr = gr_ref[0]
    row = lax.broadcasted_iota(I32, (L, L), 0)
    col = lax.broadcasted_iota(I32, (L, L), 1)
    causal = col <= row
    b_col = jnp.dot(causal.astype(F32), gc, precision=HIGHEST, preferred_element_type=F32)
    b_row = jnp.dot(gr, (row <= col).astype(F32), precision=HIGHEST, preferred_element_type=F32)
    ones_lane0 = (lax.broadcasted_iota(I32, (L, 128), 1) == 0).astype(BF16)

    for h in range(M_HEADS):
        bc = b_col[:, M_HEADS + h:M_HEADS + h + 1]
        br = b_row[M_HEADS + h:M_HEADS + h + 1, :]
        igr = gr[h:h + 1, :]
        m_prev = m_sc[h:h + 1, 0:1]
        dlog = jnp.where(causal, bc - br + igr, NEG)
        m_inter = bc + m_prev
        m_t = jnp.maximum(m_inter, jnp.max(dlog, axis=1, keepdims=True))
        w_intra = jnp.exp(dlog - m_t)
        w_inter = jnp.exp(m_inter - m_t)

        qh = (q_ref[:, h * M_DK:(h + 1) * M_DK] * (M_DK ** -0.5)).astype(BF16)
        kth = kt_ref[0, h * M_DK:(h + 1) * M_DK, :]
        v_ext = jnp.concatenate(
            [v_ref[:, h * M_DV:(h + 1) * M_DV].astype(BF16), ones_lane0], axis=1)
        s = jnp.dot(qh, kth.astype(BF16), preferred_element_type=F32) * w_intra
        c_h = c_sc[h]
        nd = (jnp.dot(s.astype(BF16), v_ext, preferred_element_type=F32)
              + w_inter * jnp.dot(qh, c_h.astype(BF16), preferred_element_type=F32))
        num = nd[:, :M_DV]
        den = nd[:, M_DV:M_DV + 1]
        hh = num / jnp.maximum(jnp.abs(den), jnp.exp(-m_t))
        hh = hh * lax.rsqrt(jnp.mean(hh * hh, axis=1, keepdims=True) + 1e-6)
        og = o_ref[:, h * M_DV:(h + 1) * M_DV]
        hn_ref[:, h * M_DV:(h + 1) * M_DV] = (
            hh * gain_ref[:, h * M_DV:(h + 1) * M_DV] * (1.0 / (1.0 + jnp.exp(-og))))

        m_new = m_t[L - 1:L, :]
        b_last = bc[L - 1:L, :]
        decay = jnp.exp(b_last + m_prev - m_new)
        w_s = jnp.exp(b_last - br + igr - m_new)
        kw = (kth * w_s).astype(BF16)
        c_sc[h] = decay * c_h + jnp.dot(kw, v_ext, preferred_element_type=F32)
        m_sc[h:h + 1, :] = jnp.broadcast_to(m_new, (1, 128))

    @pl.when(c == pl.num_programs(1) - 1)
    def _():
        cout_ref[0] = c_sc[...]
        mout_ref[0] = m_sc[...]


def _mlstm(proj, kt, gates, gates_t, c0_ext, m0_b, gain, nb, seq, chunk):
    nc = seq // chunk
    qk = M_HEADS * M_DK
    vd = M_HEADS * M_DV
    row_blk = lambda b, c: (b * nc + c, 0)
    return pl.pallas_call(
        _mlstm_kernel,
        grid=(nb, nc),
        in_specs=[pl.BlockSpec((chunk, qk), row_blk),
                  pl.BlockSpec((1, qk, chunk), lambda b, c: (b, 0, c)),
                  pl.BlockSpec((chunk, vd), lambda b, c: (b * nc + c, 1)),
                  pl.BlockSpec((chunk, vd), lambda b, c: (b * nc + c, 2)),
                  pl.BlockSpec((chunk, 128), row_blk),
                  pl.BlockSpec((1, 2 * M_HEADS, chunk), lambda b, c: (b, 0, c)),
                  pl.BlockSpec((1, M_HEADS, M_DK, 2 * M_DV), lambda b, c: (b, 0, 0, 0)),
                  pl.BlockSpec((1, M_HEADS, 128), lambda b, c: (b, 0, 0)),
                  pl.BlockSpec((1, vd), lambda b, c: (0, 0))],
        out_specs=[pl.BlockSpec((chunk, vd), row_blk),
                   pl.BlockSpec((1, M_HEADS, M_DK, 2 * M_DV), lambda b, c: (b, 0, 0, 0)),
                   pl.BlockSpec((1, M_HEADS, 128), lambda b, c: (b, 0, 0))],
        out_shape=[jax.ShapeDtypeStruct((nb * seq, vd), F32),
                   jax.ShapeDtypeStruct((nb, M_HEADS, M_DK, 2 * M_DV), F32),
                   jax.ShapeDtypeStruct((nb, M_HEADS, 128), F32)],
        scratch_shapes=[pltpu.VMEM((M_HEADS, M_DK, 2 * M_DV), F32),
                        pltpu.VMEM((M_HEADS, 128), F32)],
        compiler_params=_cparams(("parallel", "arbitrary")),
        name="mlstm",
    )(proj, kt, proj, proj, gates, gates_t, c0_ext, m0_b, gain.reshape(1, vd))


def _mlstm_layer(x_all, n_p, nb_p, seq_p, nb_s, seq_s, st_c, st_n, st_m, w_in, b_gate, gain):
    nt = x_all.shape[0]
    qk = M_HEADS * M_DK
    vd = M_HEADS * M_DV
    n_s = nb_s * seq_s
    proj = _matmul(x_all, w_in[:, :2 * qk + 2 * vd].astype(BF16))
    gates = _mlstm_gates(x_all, w_in[:, 2 * qk + 2 * vd:], b_gate)

    kt_p = proj[:n_p, qk:2 * qk].reshape(nb_p, seq_p, qk).transpose(0, 2, 1)
    gt_p = gates[:n_p, :2 * M_HEADS].reshape(nb_p, seq_p, 2 * M_HEADS).transpose(0, 2, 1)
    c0_p = jnp.zeros((nb_p, M_HEADS, M_DK, 2 * M_DV), F32)
    m0_p = jnp.zeros((nb_p, M_HEADS, 128), F32)
    hn_p, c_p, m_p = _mlstm(proj, kt_p, gates, gt_p, c0_p, m0_p, gain, nb_p, seq_p,
                            min(MLSTM_CHUNK, seq_p))

    ls = MLSTM_SAMPLE_CHUNK
    pad = ((0, 0), (0, ls - seq_s), (0, 0))
    proj_s = jnp.pad(proj[n_p:n_p + n_s].reshape(nb_s, seq_s, -1), pad)
    g_s = gates[n_p:n_p + n_s].reshape(nb_s, seq_s, 128)
    pad_row = jnp.where(jnp.arange(128) < M_HEADS, NEG, 0.0).astype(F32)
    g_s = jnp.concatenate([g_s, jnp.broadcast_to(pad_row, (nb_s, ls - seq_s, 128))], axis=1)
    kt_s = proj_s[:, :, qk:2 * qk].transpose(0, 2, 1)
    gt_s = g_s[:, :, :2 * M_HEADS].transpose(0, 2, 1)
    c0_s = jnp.concatenate([st_c, st_n[..., None],
                            jnp.zeros(st_c.shape[:3] + (M_DV - 1,), F32)], axis=-1)
    m0_s = jnp.broadcast_to(st_m[..., None], st_m.shape + (128,))
    hn_s, c_s, m_s = _mlstm(proj_s.reshape(nb_s * ls, -1), kt_s, g_s.reshape(nb_s * ls, 128), gt_s,
                            c0_s, m0_s, gain, nb_s, ls, ls)
    hn_s = hn_s.reshape(nb_s, ls, vd)[:, :seq_s].reshape(n_s, vd)
    hn = jnp.concatenate([hn_p, hn_s, jnp.zeros((nt - n_p - n_s, vd), F32)], axis=0)
    states = (c_p[..., :M_DV], c_p[..., M_DV], m_p[..., 0],
              c_s[..., :M_DV], c_s[..., M_DV], m_s[..., 0])
    return hn, states


def _rope_kernel(q_ref, k_ref, ca_ref, cb_ref, cc_ref, qo_ref, ko_ref, km_ref):
    ca, cb, cc = ca_ref[...], cb_ref[...], cc_ref[...]
    half = ROT_DIM // 2
    for j in range(q_ref.shape[1] // 128):
        sl = slice(j * 128, (j + 1) * 128)
        for src, dst in ((q_ref, qo_ref), (k_ref, ko_ref)):
            x = src[:, sl]
            dst[:, sl] = (x * ca + pltpu.roll(x, 128 - half, 1) * cb + pltpu.roll(x, half, 1) * cc)
    km_ref[0] = jnp.mean(ko_ref[...], axis=0, keepdims=True)


def _rope_tables(pos):
    half = ROT_DIM // 2
    inv = ROPE_THETA ** (-jnp.arange(0, ROT_DIM, 2, dtype=F32) / ROT_DIM)
    ang = pos.astype(F32)[:, None] * inv[None, :]
    cos, sin = jnp.cos(ang), jnp.sin(ang)
    lane = jnp.arange(128) % A_HD
    f = lane % half
    in_lo = lane < half
    in_hi = (lane >= half) & (lane < ROT_DIM)
    ca = jnp.where((in_lo | in_hi)[None, :], cos[:, f], 1.0)
    cb = jnp.where(in_lo[None, :], -sin[:, f], 0.0)
    cc = jnp.where(in_hi[None, :], sin[:, f], 0.0)
    return ca.astype(F32), cb.astype(F32), cc.astype(F32)


def _rope(qkv, tables, d):
    nt = qkv.shape[0]
    tm = MOBA_BLOCK
    spec_t = pl.BlockSpec((tm, 128), lambda i: (i, 0))
    return pl.pallas_call(
        _rope_kernel,
        grid=(nt // tm,),
        in_specs=[pl.BlockSpec((tm, d), lambda i: (i, 0)),
                  pl.BlockSpec((tm, d), lambda i: (i, 1)),
                  spec_t, spec_t, spec_t],
        out_specs=[pl.BlockSpec((tm, d), lambda i: (i, 0)),
                   pl.BlockSpec((tm, d), lambda i: (i, 0)),
                   pl.BlockSpec((1, 1, d), lambda i: (i, 0, 0))],
        out_shape=[jax.ShapeDtypeStruct((nt, d), F32),
                   jax.ShapeDtypeStruct((nt, d), F32),
                   jax.ShapeDtypeStruct((nt // tm, 1, d), F32)],
        compiler_params=_cparams(("parallel",)),
        name="rope",
    )(qkv, qkv, *tables)


def _moba_prompt_kernel(q_ref, k_ref, vt_ref, km_ref, o_ref):
    qi = pl.program_id(2)
    blk = MOBA_BLOCK
    nblk = km_ref.shape[0]
    q = q_ref[...] * (A_HD ** -0.5)
    lane = lax.broadcasted_iota(I32, (blk, 128), 1)
    key_i = lax.broadcasted_iota(I32, (blk, blk), 0)
    qry_i = lax.broadcasted_iota(I32, (blk, blk), 1)
    jrow = lax.broadcasted_iota(I32, (nblk, blk), 0)
    start = pl.multiple_of(qi * blk, blk)
    k_own = k_ref[pl.ds(start, blk), :].astype(BF16)
    vt_own = vt_ref[qi].astype(BF16)

    qbs, sels, init = [], [], []
    for hh in range(2):
        qh = jnp.where((lane // A_HD) == hh, q, 0.0)
        qb = qh.astype(BF16)
        gate = lax.dot_general(km_ref[...], qh, NT_DIMS, precision=HIGHEST,
                               preferred_element_type=F32)
        valid = jrow < qi
        gm = jnp.where(valid, gate, NEG)
        rank = jnp.zeros((nblk, blk), I32)
        for jp in range(nblk - 1):
            gj = gm[jp:jp + 1, :]
            beats = (gj > gm) | ((gj == gm) & (jp < jrow))
            rank = rank + jnp.where(beats, 1, 0)
        sels.append(jnp.where(valid & (rank < MOBA_TOPK), 1.0, 0.0))
        qbs.append(qb)

        s = lax.dot_general(k_own, qb, NT_DIMS, preferred_element_type=F32)
        s = jnp.where(key_i <= qry_i, s, NEG)
        m0 = jnp.max(s, axis=0, keepdims=True)
        p = jnp.exp(s - m0)
        init += [m0, jnp.sum(p, axis=0, keepdims=True),
                 jnp.dot(vt_own, p.astype(BF16), preferred_element_type=F32)]

    def scores(j):
        kj = k_ref[pl.ds(pl.multiple_of(j * blk, blk), blk), :].astype(BF16)
        return [lax.dot_general(kj, qbs[hh], NT_DIMS, preferred_element_type=F32) for hh in range(2)]

    def body(j, carry):
        s_next = scores(jnp.minimum(j + 1, qi))
        vtj = vt_ref[j].astype(BF16)
        new = []
        for hh in range(2):
            m, l, acc = carry[3 * hh:3 * hh + 3]
            selj = jnp.sum(jnp.where(jrow == j, sels[hh], 0.0), axis=0, keepdims=True)
            sj = jnp.where(selj > 0.0, carry[6 + hh], NEG)
            m_new = jnp.maximum(m, jnp.max(sj, axis=0, keepdims=True))
            alpha = jnp.exp(m - m_new)
            pj = jnp.exp(sj - m_new)
            new += [m_new, alpha * l + jnp.sum(pj, axis=0, keepdims=True),
                    alpha * acc + jnp.dot(vtj, pj.astype(BF16), preferred_element_type=F32)]
        return tuple(new + s_next)

    fin = lax.fori_loop(0, qi, body, tuple(init + scores(0)))
    drow = lax.broadcasted_iota(I32, (128, blk), 0)
    out_t = jnp.where(drow < A_HD, fin[2] / fin[1], fin[5] / fin[4])
    o_ref[...] = out_t.T


def _moba_prompt(q_rot, k_rot, qkv, kmean, nb, seq, d):
    nq = seq // MOBA_BLOCK
    ng = d // 128
    v_t = qkv[:nb * seq, 2 * d:].reshape(nb * nq, MOBA_BLOCK, d).transpose(0, 2, 1)
    return pl.pallas_call(
        _moba_prompt_kernel,
        grid=(nb, ng, nq),
        in_specs=[pl.BlockSpec((MOBA_BLOCK, 128), lambda b, g, i: (b * nq + i, g)),
                  pl.BlockSpec((seq, 128), lambda b, g, i: (b, g)),
                  pl.BlockSpec((nq, 128, MOBA_BLOCK), lambda b, g, i: (b, g, 0)),
                  pl.BlockSpec((nq, 128), lambda b, g, i: (b, g))],
        out_specs=pl.BlockSpec((MOBA_BLOCK, 128), lambda b, g, i: (b * nq + i, g)),
        out_shape=jax.ShapeDtypeStruct((nb * seq, d), F32),
        compiler_params=_cparams(("parallel", "parallel", "arbitrary")),
        name="moba_prompt",
    )(q_rot, k_rot, v_t, kmean)


HEAD_BATCH = (((2,), (1,)), ((0,), (0,)))
HEAD_BATCH_NT = (((2,), (2,)), ((0,), (0,)))
T_PAD = 8


def _moba_sample_kernel(pt_ref, q_ref, kn_ref, vn_ref, k0_ref, k1_ref, v0_ref, v1_ref, o_ref,
                        m_sc, l_sc, g_sc, acc_sc):
    j = pl.program_id(1)
    n_blk = acc_sc.shape[0]
    stat_shape = m_sc.shape

    @pl.when(j == 0)
    def _():
        m_sc[...] = jnp.full(stat_shape, NEG, F32)
        l_sc[...] = jnp.zeros(stat_shape, F32)
        g_sc[...] = jnp.zeros(stat_shape, F32)

    qb = (q_ref[0] * (A_HD ** -0.5)).astype(BF16)
    s = jnp.concatenate(
        [lax.dot_general(qb, kr[0].astype(BF16), HEAD_BATCH, preferred_element_type=F32)
         for kr in (k0_ref, k1_ref)], axis=-1)
    m = jnp.max(s, axis=-1, keepdims=True)
    pw = jnp.exp(s - m)
    half = PAGE_SIZE
    acc_sc[j] = (lax.dot_general(pw[..., :half].astype(BF16), v0_ref[0].astype(BF16), HEAD_BATCH_NT,
                                 preferred_element_type=F32)
                 + lax.dot_general(pw[..., half:].astype(BF16), v1_ref[0].astype(BF16), HEAD_BATCH_NT,
                                   preferred_element_type=F32))
    col = lax.broadcasted_iota(I32, stat_shape, 2) == j
    m_sc[...] = jnp.where(col, m, m_sc[...])
    l_sc[...] = jnp.where(col, jnp.sum(pw, axis=-1, keepdims=True), l_sc[...])
    g_sc[...] = jnp.where(col, jnp.sum(s, axis=-1, keepdims=True), g_sc[...])

    @pl.when(j == n_blk - 1)
    def _():
        jidx = lax.broadcasted_iota(I32, stat_shape, 2)
        gate = jnp.where(jidx < n_blk, g_sc[...], -jnp.inf)
        sel = jnp.zeros(stat_shape, F32)
        for _ in range(min(MOBA_TOPK, n_blk)):
            gmax = jnp.max(gate, axis=-1, keepdims=True)
            first = jnp.min(jnp.where(gate == gmax, jidx, n_blk), axis=-1, keepdims=True)
            hit = jidx == first
            sel = jnp.where(hit, 1.0, sel)
            gate = jnp.where(hit, -jnp.inf, gate)
        so = lax.dot_general(qb, kn_ref[0].astype(BF16), HEAD_BATCH_NT, preferred_element_type=F32)
        t_q = lax.broadcasted_iota(I32, so.shape, 1)
        t_k = lax.broadcasted_iota(I32, so.shape, 2)
        so = jnp.where(t_k <= t_q, so, NEG)
        m_blk = jnp.where(sel > 0.0, m_sc[...], NEG)
        m_tot = jnp.maximum(jnp.max(so, axis=-1, keepdims=True), jnp.max(m_blk, axis=-1, keepdims=True))
        po = jnp.exp(so - m_tot)
        wgt = jnp.where(sel > 0.0, jnp.exp(m_blk - m_tot), 0.0)
        l_tot = jnp.sum(po, axis=-1, keepdims=True) + jnp.sum(wgt * l_sc[...], axis=-1, keepdims=True)
        acc = lax.dot_general(po.astype(BF16), vn_ref[0].astype(BF16), HEAD_BATCH, preferred_element_type=F32)
        for jb in range(n_blk):
            acc = acc + wgt[:, :, jb:jb + 1] * acc_sc[jb]
        o_ref[0] = acc / l_tot


def _moba_sample(q_h, k_h, v_h, cache_kt, cache_vt, page_table_flat):
    nb, nh, t_pad, hd = q_h.shape
    ppb = MOBA_BLOCK // PAGE_SIZE
    assert ppb == 2
    n_pages = page_table_flat.shape[0] // nb
    n_blk = n_pages // ppb
    lanes = -(-n_blk // 128) * 128
    seq_map = lambda b, j, pt: (b, 0, 0, 0)
    page_spec = lambda r: pl.BlockSpec((1, nh, hd, PAGE_SIZE),
                                       lambda b, j, pt: (pt[b * n_pages + j * ppb + r], 0, 0, 0))
    grid_spec = pltpu.PrefetchScalarGridSpec(
        num_scalar_prefetch=1,
        grid=(nb, n_blk),
        in_specs=[pl.BlockSpec((1, nh, t_pad, hd), seq_map),
                  pl.BlockSpec((1, nh, t_pad, hd), seq_map),
                  pl.BlockSpec((1, nh, t_pad, hd), seq_map),
                  page_spec(0), page_spec(1), page_spec(0), page_spec(1)],
        out_specs=pl.BlockSpec((1, nh, t_pad, hd), seq_map),
        scratch_shapes=[pltpu.VMEM((nh, t_pad, lanes), F32),
                        pltpu.VMEM((nh, t_pad, lanes), F32),
                        pltpu.VMEM((nh, t_pad, lanes), F32),
                        pltpu.VMEM((n_blk, nh, t_pad, hd), F32)],
    )
    return pl.pallas_call(
        _moba_sample_kernel,
        grid_spec=grid_spec,
        out_shape=jax.ShapeDtypeStruct((nb, nh, t_pad, hd), F32),
        compiler_params=_cparams(("parallel", "arbitrary")),
        name="moba_sample",
    )(page_table_flat, q_h, k_h, v_h, cache_kt, cache_kt, cache_vt, cache_vt)


def _top16_desc(vals, payload, n_rows):
    t = vals.shape[1]
    r16 = lax.broadcasted_iota(I32, (P_TOPK, t), 0)
    out_v = jnp.zeros((P_TOPK, t), F32)
    out_p = jnp.zeros((P_TOPK, t), I32)
    big = jnp.int32(2 ** 30)
    for r in range(P_TOPK):
        m = jnp.max(vals, axis=0, keepdims=True)
        pick = jnp.min(jnp.where(vals == m, payload, big), axis=0, keepdims=True)
        vals = jnp.where(payload == pick, -jnp.inf, vals)
        out_v = jnp.where(r16 == r, m, out_v)
        out_p = jnp.where(r16 == r, pick, out_p)
    return out_v, out_p


def _sort16_pairs():
    res = []

    def merge(lo, hi, r):
        step = r * 2
        if step < hi - lo:
            merge(lo, hi, step)
            merge(lo + r, hi, step)
            for i in range(lo + r, hi - r, step):
                res.append((i, i + r))
        else:
            res.append((lo, lo + r))

    def sort(lo, hi):
        if hi - lo >= 1:
            mid = lo + (hi - lo) // 2
            sort(lo, mid)
            sort(mid + 1, hi)
            merge(lo, hi, 1)

    sort(0, P_TOPK - 1)
    return res


def _top16_of_128(vals, payload):
    t = vals.shape[1]
    v = [vals[8 * i:8 * i + 8] for i in range(P_TOPK)]
    p = [payload[8 * i:8 * i + 8] for i in range(P_TOPK)]
    for a, b in _sort16_pairs():
        swap = v[b] > v[a]
        v[a], v[b] = jnp.where(swap, v[b], v[a]), jnp.where(swap, v[a], v[b])
        p[a], p[b] = jnp.where(swap, p[b], p[a]), jnp.where(swap, p[a], p[b])
    sub = lax.broadcasted_iota(I32, (8, t), 0)
    r16 = lax.broadcasted_iota(I32, (P_TOPK, t), 0)
    out_v = jnp.zeros((P_TOPK, t), F32)
    out_p = jnp.zeros((P_TOPK, t), I32)
    for r in range(P_TOPK):
        m = jnp.max(v[0], axis=0, keepdims=True)
        first = jnp.min(jnp.where(v[0] == m, sub, 8), axis=0, keepdims=True)
        hit = sub == first
        pick = jnp.max(jnp.where(hit, p[0], -1), axis=0, keepdims=True)
        out_v = jnp.where(r16 == r, m, out_v)
        out_p = jnp.where(r16 == r, pick, out_p)
        depth = P_TOPK - r
        for k in range(depth - 1):
            v[k] = jnp.where(hit, v[k + 1], v[k])
            p[k] = jnp.where(hit, p[k + 1], p[k])
        v[depth - 1] = jnp.where(hit, -jnp.inf, v[depth - 1])
    return out_v, out_p


def _peer_topk_kernel(q_ref, keys_ref, pidx_ref, g_ref):
    t = q_ref.shape[0]
    kio = lax.broadcasted_iota(I32, (P_NKEYS, t), 0)
    e_heads, g_heads = [], []
    for h in range(P_HEADS):
        sv, si = [], []
        for p in range(2):
            c0 = (h * 2 + p) * (P_NKEYS)
            scores = lax.dot_general(keys_ref[h, p], q_ref[:, c0:c0 + 128], NT_DIMS,
                                     precision=HIGHEST, preferred_element_type=F32)
            v, i = _top16_of_128(scores, kio)
            sv.append(v)
            si.append(i)
        half = P_TOPK // 2
        cand = jnp.concatenate(
            [sv[0][0:1, :] + sv[1]]
            + [sv[0][i:i + 1, :] + sv[1][0:half, :] for i in range(1, half)]
            + [sv[0][half:, :] + sv[1][0:1, :]], axis=0)
        eid = jnp.concatenate(
            [si[0][0:1, :] * P_NKEYS + si[1]]
            + [si[0][i:i + 1, :] * P_NKEYS + si[1][0:half, :] for i in range(1, half)]
            + [si[0][half:, :] * P_NKEYS + si[1][0:1, :]], axis=0)
        cv, ce = _top16_desc(cand, eid, cand.shape[0])
        ex = jnp.exp(cv - cv[0:1, :])
        g_heads.append(ex / jnp.sum(ex, axis=0, keepdims=True))
        e_heads.append(ce)
    e_all = jnp.concatenate(e_heads, axis=0).T
    pidx_ref[...] = e_all
    g_ref[...] = jnp.concatenate(g_heads, axis=0).T


def _peer_topk(q, keys):
    nt = q.shape[0]
    t = TOK_BLOCK
    return pl.pallas_call(
        _peer_topk_kernel,
        grid=(nt // t,),
        in_specs=[pl.BlockSpec((t, q.shape[1]), lambda i: (i, 0)),
                  pl.BlockSpec(keys.shape, lambda i: (0, 0, 0, 0))],
        out_specs=[pl.BlockSpec((t, P_SLOTS), lambda i: (i, 0)),
                   pl.BlockSpec((t, P_SLOTS), lambda i: (i, 0))],
        out_shape=[jax.ShapeDtypeStruct((nt, P_SLOTS), I32),
                   jax.ShapeDtypeStruct((nt, P_SLOTS), F32)],
        compiler_params=_cparams(("parallel",)),
        name="peer_topk",
    )(q, keys)


GATHER_CHUNK = 32
TILE_ROWS = 8
SLOT_COLS = P_SLOTS * TILE_ROWS


def _gather_tiles(tab_ref, idx_smem, slot, j, c):
    return jnp.concatenate(
        [tab_ref[idx_smem[slot, j, c * GATHER_CHUNK + i]] for i in range(GATHER_CHUNK)], axis=0)


def _index_copy(pidx_ref, idx_smem, sem, g8, slot):
    return pltpu.make_async_copy(pidx_ref.at[pl.ds(pl.multiple_of(g8 * 8, 8), 8), :],
                                 idx_smem.at[slot], sem.at[slot])


def _for_token_groups(pidx_ref, idx_smem, sem, n_groups, body):
    _index_copy(pidx_ref, idx_smem, sem, 0, 0).start()

    def pair(gp, carry):
        for slot in range(2):
            g8 = gp * 2 + slot
            _index_copy(pidx_ref, idx_smem, sem, g8, slot).wait()

            @pl.when(g8 + 1 < n_groups)
            def _():
                _index_copy(pidx_ref, idx_smem, sem, g8 + 1, 1 - slot).start()

            body(g8, slot)
        return carry

    lax.fori_loop(0, n_groups // 2, pair, 0)


def _split_bf16(x):
    hi = x.astype(BF16)
    lo = (x - hi.astype(F32)).astype(BF16)
    return jnp.concatenate([hi, lo], axis=0)


def _fold_row_mask(width):
    r = lax.broadcasted_iota(I32, (8, width), 1) % TILE_ROWS
    c = lax.broadcasted_iota(I32, (8, width), 0)
    return r == c


def _peer_u_kernel(pidx_ref, xf_ref, g_ref, tab_ref, exp_ref, w_ref, hid_sc, idx_smem, sem):
    tb = g_ref.shape[0]
    n_chunk = P_SLOTS // GATHER_CHUNK
    cw = GATHER_CHUNK * TILE_ROWS
    keep = _fold_row_mask(cw)

    def group(g8, slot):
        base = pl.multiple_of(g8 * 8, 8)
        rows = []
        for j in range(8):
            n = base + j
            x16 = _split_bf16(xf_ref[pl.ds(pl.multiple_of(n * 8, 8), 8), :])
            parts = []
            for c in range(n_chunk):
                y = lax.dot_general(x16, _gather_tiles(tab_ref, idx_smem, slot, j, c), NT_DIMS,
                                    preferred_element_type=F32)
                parts.append(jnp.sum(jnp.where(keep, y[0:8] + y[8:16], 0.0), axis=0, keepdims=True))
            rows.append(jnp.concatenate(parts, axis=1))
        s16 = _split_bf16(jnp.concatenate(rows, axis=0))
        h = lax.dot_general(s16, exp_ref[...], NT_DIMS, preferred_element_type=F32)
        hid_sc[pl.ds(base, 8), :] = h[0:8] + h[8:16]

    _for_token_groups(pidx_ref, idx_smem, sem, tb // 8, group)
    hid = hid_sc[...]
    gelu = 0.5 * hid * (1.0 + lax.erf(hid * (2.0 ** -0.5)))
    w_ref[...] = g_ref[...] * gelu


def _peer_v_kernel(pidx_ref, w_ref, tab_ref, exp_ref, o_ref, wx_sc, idx_smem, sem):
    tb = w_ref.shape[0]
    n_chunk = P_SLOTS // GATHER_CHUNK
    cw = GATHER_CHUNK * TILE_ROWS
    keep = _fold_row_mask(SLOT_COLS)
    w = w_ref[...]
    w_hi = w.astype(BF16)
    w_lo = (w - w_hi.astype(F32)).astype(BF16)
    wx_sc[...] = (jnp.dot(w_hi, exp_ref[...], preferred_element_type=F32)
                  + jnp.dot(w_lo, exp_ref[...], preferred_element_type=F32))

    def group(g8, slot):
        base = pl.multiple_of(g8 * 8, 8)
        w_blk = wx_sc[pl.ds(base, 8), :]
        for j in range(8):
            n = base + j
            lhs =_split_bf16(jnp.where(keep, w_blk[j:j + 1, :], 0.0))
            acc = jnp.zeros((16, 128), F32)
            for c in range(n_chunk):
                acc = acc + jnp.dot(lhs[:, c * cw:(c + 1) * cw], _gather_tiles(tab_ref, idx_smem, slot, j, c),
                                    preferred_element_type=F32)
            o_ref[pl.ds(pl.multiple_of(n * 8, 8), 8), :] = acc[0:8] + acc[8:16]

    _for_token_groups(pidx_ref, idx_smem, sem, tb // 8, group)


def _peer_gather(x, pidx, g, u_rows, v_rows):
    nt, d = x.shape
    fold = d // 128
    tb = TOK_BLOCK
    tab_spec = pl.BlockSpec(u_rows.shape, lambda i: (0, 0, 0), pipeline_mode=pl.Buffered(1))
    slot_spec = pl.BlockSpec((tb, P_SLOTS), lambda i: (i, 0))
    idx_staging = [pltpu.SMEM((2, 8, P_SLOTS), I32), pltpu.SemaphoreType.DMA((2,))]
    exp_spec = pl.BlockSpec((P_SLOTS, SLOT_COLS), lambda i: (0, 0))
    expand = (jnp.arange(SLOT_COLS)[None, :] // TILE_ROWS == jnp.arange(P_SLOTS)[:, None]).astype(BF16)
    w = pl.pallas_call(
        _peer_u_kernel,
        grid=(nt // tb,),
        in_specs=[slot_spec, pl.BlockSpec((tb * fold, 128), lambda i: (i, 0)), slot_spec, tab_spec, exp_spec],
        out_specs=slot_spec,
        out_shape=jax.ShapeDtypeStruct((nt, P_SLOTS), F32),
        scratch_shapes=[pltpu.VMEM((tb, P_SLOTS), F32)] + idx_staging,
        compiler_params=_cparams(("parallel",)),
        name="peer_u",
    )(pidx, x.reshape(nt * fold, 128), g, u_rows, expand)
    out = pl.pallas_call(
        _peer_v_kernel,
        grid=(nt // tb,),
        in_specs=[slot_spec, slot_spec, tab_spec, exp_spec],
        out_specs=pl.BlockSpec((tb * fold, 128), lambda i: (i, 0)),
        out_shape=jax.ShapeDtypeStruct((nt * fold, 128), F32),
        scratch_shapes=[pltpu.VMEM((tb, SLOT_COLS), F32)] + idx_staging,
        compiler_params=_cparams(("parallel",)),
        name="peer_v",
    )(pidx, w, v_rows, expand)
    return out.reshape(nt, d)


def _fold_table(tab):
    e, d = tab.shape
    return tab.astype(BF16).reshape(e, d // 128, 128)


def _peer_layer(x, w_q, keys, u_tab, v_tab, ln_g, ln_b):
    q = _matmul(x, w_q.astype(BF16))
    pidx, g = _peer_topk(q, keys)
    y = _peer_gather(x, pidx, g, _fold_table(u_tab), _fold_table(v_tab))
    return _res_ln(x, y, ln_g, ln_b)


def kernel(x_prompt, x_sample, state_mlstm_C, state_mlstm_n, state_mlstm_m, cache_k, cache_v, page_table,
           w_in_a, b_gate_a, mh_norm_a, w_out_a, w_kv, w_q_b, w_out_b,
           ln_mix_g, ln_mix_b, ln_ffn_g, ln_ffn_b, peer_wq, peer_keys, peer_u, peer_v):
    nb_p, seq_p, d = x_prompt.shape
    nb_s, seq_s, _ = x_sample.shape
    n_p, n_s = nb_p * seq_p, nb_s * seq_s
    nt = -(-(n_p + n_s) // ROW_TILE) * ROW_TILE
    past_len = page_table.shape[1] * PAGE_SIZE
    ppb = MOBA_BLOCK // PAGE_SIZE
    assert past_len % MOBA_BLOCK == 0 and seq_p % MOBA_BLOCK == 0 and n_p % ROW_TILE == 0
    assert state_mlstm_C.shape[0] == 1 and w_q_b.shape[0] == 1

    x = jnp.concatenate([x_prompt.reshape(n_p, d), x_sample.reshape(n_s, d),
                         jnp.zeros((nt - n_p - n_s, d), F32)], axis=0)

    hn, states = _mlstm_layer(x, n_p, nb_p, seq_p, nb_s, seq_s,
                              state_mlstm_C[0], state_mlstm_n[0], state_mlstm_m[0],
                              w_in_a[0], b_gate_a[0], mh_norm_a[0])
    x = _matmul_res_ln(hn, w_out_a[0].astype(BF16), x, ln_mix_g[0], ln_mix_b[0])
    x = _peer_layer(x, peer_wq[0], peer_keys[0], peer_u[0], peer_v[0], ln_ffn_g[0], ln_ffn_b[0])

    qkv = _matmul(x, jnp.concatenate([w_q_b[0], w_kv], axis=1).astype(BF16))
    pos = jnp.concatenate([jnp.tile(jnp.arange(seq_p), nb_p),
                           past_len + jnp.tile(jnp.arange(seq_s), nb_s),
                           jnp.zeros((nt - n_p - n_s,), I32)])
    q_rot, k_rot, kmean = _rope(qkv, _rope_tables(pos), d)
    attn_p = _moba_prompt(q_rot, k_rot, qkv, kmean.reshape(nt // MOBA_BLOCK, d), nb_p, seq_p, d)

    q_s = q_rot[n_p:n_p + n_s].reshape(nb_s, seq_s, d)
    k_s = k_rot[n_p:n_p + n_s].reshape(nb_s, seq_s, d)
    v_s = qkv[n_p:n_p + n_s, 2 * d:].reshape(nb_s, seq_s, d)
    pt_flat = page_table.reshape(-1).astype(I32)

    def head_major(a):
        a = a.reshape(nb_s, seq_s, A_HEADS, A_HD).transpose(0, 2, 1, 3)
        return jnp.pad(a, ((0, 0), (0, 0), (0, T_PAD - seq_s), (0, 0)))

    attn_s = _moba_sample(head_major(q_s), head_major(k_s), head_major(v_s),
                          cache_k.transpose(0, 2, 3, 1), cache_v.transpose(0, 2, 3, 1), pt_flat)
    attn_s = attn_s[:, :, :seq_s, :].transpose(0, 2, 1, 3)

    attn = jnp.concatenate([attn_p, attn_s.reshape(n_s, d), jnp.zeros((nt - n_p - n_s, d), F32)], axis=0)
    x = _matmul_res_ln(attn, w_out_b[0].astype(BF16), x, ln_mix_g[1], ln_mix_b[1])
    x = _peer_layer(x, peer_wq[1], peer_keys[1], peer_u[1], peer_v[1], ln_ffn_g[1], ln_ffn_b[1])

    y_prompt = x[:n_p].reshape(nb_p, seq_p, d)
    y_sample = x[n_p:n_p + n_s].reshape(nb_s, seq_s, d)
    c_p, n_pr, m_p, c_s, n_sm, m_s = states
    k_prompt = k_rot[:n_p].reshape(nb_p, seq_p, A_HEADS, A_HD)
    v_prompt = qkv[:n_p, 2 * d:].reshape(nb_p, seq_p, A_HEADS, A_HD)
    return (y_prompt, y_sample, c_p[None], n_pr[None], m_p[None], c_s[None], n_sm[None], m_s[None],
            k_prompt, v_prompt, k_s.reshape(nb_s, seq_s, A_HEADS, A_HD), v_s.reshape(nb_s, seq_s, A_HEADS, A_HD))
```
